```python
import math
import jax, jax.numpy as jnp
from jax import lax
import numpy as np

D_MODEL = 1024
BATCH = 2
SEQ = 8192
DEPTH = 4
DEC_BATCH = 128
DEC_SEQ = 1
PAST_LEN = 2048
PAGE_SIZE = 128

N_MIXERS = 2
N_POOL_LAYERS = (DEPTH + 1) // 2
N_ATTN_LAYERS = DEPTH // 2
POOL_EXPAND = 2
POOL_WIDTH = POOL_EXPAND * D_MODEL
POOL_WINDOWS = (2, 4, 8, 16)
N_POOL_GROUPS = len(POOL_WINDOWS)
POOL_GROUP_WIDTH = POOL_WIDTH // N_POOL_GROUPS
POOL_STATE_LEN = max(POOL_WINDOWS) - 1
N_HEADS = 8
HEAD_DIM = D_MODEL // (2 * N_HEADS)
V_HEAD_DIM = 2 * HEAD_DIM
ATTN_WIDTH = N_HEADS * V_HEAD_DIM
ROPE_THETA = 10000.0
Q_BLOCK = 128
RMS_EPS = 1e-6
NEG_INF = -1e30

kernel_name = "hybrid_pool_diffattn_decode_step"


def rms_norm(x, g):
    x32 = x.astype(jnp.float32)
    y = x32 * lax.rsqrt(jnp.mean(x32 * x32, axis=-1, keepdims=True) + RMS_EPS)
    return (y * g.astype(jnp.float32)).astype(x.dtype)


def rope(x, pos):
    half = HEAD_DIM // 2
    inv = jnp.power(ROPE_THETA, -jnp.arange(0, HEAD_DIM, 2, dtype=jnp.float32) / HEAD_DIM)
    ang = pos.astype(jnp.float32)[:, None] * inv[None, :]
    cos = jnp.cos(ang)[:, None, None, :]
    sin = jnp.sin(ang)[:, None, None, :]
    x32 = x.astype(jnp.float32)
    x1, x2 = x32[..., :half], x32[..., half:]
    out = jnp.concatenate([x1 * cos - x2 * sin, x2 * cos + x1 * sin], axis=-1)
    return out.astype(x.dtype)


def causal_multiscale_pool(u, pos):
    B, L, _ = u.shape
    g = u.astype(jnp.float32).reshape(B, L, N_POOL_GROUPS, POOL_GROUP_WIDTH)
    cs = jnp.cumsum(g, axis=1)
    outs = []
    for gi, w in enumerate(POOL_WINDOWS):
        c = cs[:, :, gi]
        lagged = jnp.pad(c, ((0, 0), (w, 0), (0, 0)))[:, :L]
        cnt = jnp.minimum(pos + 1, w).astype(jnp.float32)[None, :, None]
        outs.append((c - lagged) / cnt - g[:, :, gi])
    return jnp.stack(outs, axis=2)


def pool_layer(x, u_past, start_pos, norm_g, w_in, w_grp, scale, w_out):
    B, S, _ = x.shape
    P = u_past.shape[1]
    h = rms_norm(x, norm_g)
    uz = h @ w_in
    u, z = uz[..., :POOL_WIDTH], uz[..., POOL_WIDTH:]
    u_ext = jnp.concatenate([u_past.astype(u.dtype), u], axis=1)
    pos_ext = start_pos - P + jnp.arange(P + S)
    r = causal_multiscale_pool(u_ext, pos_ext)[:, P:]
    m = jnp.einsum("bsgc,gcd->bsgd", r.astype(x.dtype), w_grp).reshape(B, S, POOL_WIDTH) * scale
    y = (m * jax.nn.silu(z)) @ w_out
    return x + y, u_ext[:, -POOL_STATE_LEN:]


def diff_attn_core(q, k, v, q_pos, k_pos, lam):
    s = jnp.einsum("bqhcd,bkhcd->bhcqk", q, k, preferred_element_type=jnp.float32) * (HEAD_DIM ** -0.5)
    mask = k_pos[None, :] <= q_pos[:, None]
    s = jnp.where(mask, s, NEG_INF)
    p = jax.nn.softmax(s, axis=-1)
    a = p[:, :, 0] - lam * p[:, :, 1]
    return jnp.einsum("bhqk,bkhe->bqhe", a.astype(v.dtype), v)


def blocked_diff_attn(q, k, v, pos, lam):
    B, S = q.shape[:2]
    nb = S // Q_BLOCK
    qb = q.reshape(B, nb, Q_BLOCK, N_HEADS, 2, HEAD_DIM).swapaxes(0, 1)
    pb = pos.reshape(nb, Q_BLOCK)
    out = lax.map(lambda a: diff_attn_core(a[0], k, v, a[1], pos, lam), (qb, pb))
    return out.swapaxes(0, 1).reshape(B, S, N_HEADS, V_HEAD_DIM)


def attn_layer(x, pos, past_k, past_v, past_pos, norm_g, w_in, qn, kn, lq1, lk1, lq2, lk2, subg, w_out, lam_init):
    B, S, _ = x.shape
    h = rms_norm(x, norm_g)
    proj = h @ w_in
    q, k, v, z = jnp.split(proj, 4, axis=-1)
    q = rope(rms_norm(q.reshape(B, S, N_HEADS, 2, HEAD_DIM), qn), pos)
    k = rope(rms_norm(k.reshape(B, S, N_HEADS, 2, HEAD_DIM), kn), pos)
    v = v.reshape(B, S, N_HEADS, V_HEAD_DIM)
    lam = (jnp.exp(jnp.sum(lq1.astype(jnp.float32) * lk1.astype(jnp.float32)))
           - jnp.exp(jnp.sum(lq2.astype(jnp.float32) * lk2.astype(jnp.float32))) + lam_init)
    if past_k is None:
        o = blocked_diff_attn(q, k, v, pos, lam)
    else:
        k_all = jnp.concatenate([past_k.astype(k.dtype), k], axis=1)
        v_all = jnp.concatenate([past_v.astype(v.dtype), v], axis=1)
        k_pos = jnp.concatenate([past_pos, pos])
        o = diff_attn_core(q, k_all, v_all, pos, k_pos, lam)
    o = rms_norm(o, subg) * (1.0 - lam_init)
    o = o.reshape(B, S, ATTN_WIDTH) * jax.nn.silu(z)
    return x + o @ w_out, k, v


def setup_inputs(seed: int = 0) -> dict:
    key = jax.random.key(seed)
    ks = jax.random.split(key, 24)
    n_pages = PAST_LEN // PAGE_SIZE
    n_used = DEC_BATCH * n_pages
    n_phys = n_used + n_used // 4
    f32 = jnp.float32
    nrm = lambda k, shp, s=1.0: jax.random.normal(k, shp, f32) * s
    page_table = jax.random.permutation(ks[0], n_phys)[:n_used].reshape(DEC_BATCH, n_pages).astype(jnp.int32)
    return {
        "x_prompt": nrm(ks[1], (BATCH, SEQ, D_MODEL)),
        "x_sample": nrm(ks[2], (DEC_BATCH, DEC_SEQ, D_MODEL)),
        "state_pool": nrm(ks[3], (N_POOL_LAYERS, DEC_BATCH, POOL_STATE_LEN, POOL_WIDTH)),
        "cache_k": nrm(ks[4], (N_ATTN_LAYERS, n_phys, PAGE_SIZE, N_HEADS, 2, HEAD_DIM)),
        "cache_v": nrm(ks[5], (N_ATTN_LAYERS, n_phys, PAGE_SIZE, N_HEADS, V_HEAD_DIM)),
        "page_table": page_table,
        "norm_pool": 1.0 + nrm(ks[6], (N_POOL_LAYERS, D_MODEL), 0.02),
        "w_in_pool": nrm(ks[7], (N_POOL_LAYERS, D_MODEL, 2 * POOL_WIDTH), D_MODEL ** -0.5),
        "w_grp_pool": nrm(ks[8], (N_POOL_LAYERS, N_POOL_GROUPS, POOL_GROUP_WIDTH, POOL_GROUP_WIDTH), POOL_GROUP_WIDTH ** -0.5),
        "scale_pool": 1.0 + nrm(ks[9], (N_POOL_LAYERS, POOL_WIDTH), 0.1),
        "w_out_pool": nrm(ks[10], (N_POOL_LAYERS, POOL_WIDTH, D_MODEL), POOL_WIDTH ** -0.5),
        "norm_attn": 1.0 + nrm(ks[11], (N_ATTN_LAYERS, D_MODEL), 0.02),
        "w_in_attn": nrm(ks[12], (N_ATTN_LAYERS, D_MODEL, 4 * D_MODEL), D_MODEL ** -0.5),
        "q_norm": 1.0 + nrm(ks[13], (N_ATTN_LAYERS, HEAD_DIM), 0.02),
        "k_norm": 1.0 + nrm(ks[14], (N_ATTN_LAYERS, HEAD_DIM), 0.02),
        "lambda_q1": nrm(ks[15], (N_ATTN_LAYERS, HEAD_DIM), 0.1),
        "lambda_k1": nrm(ks[16], (N_ATTN_LAYERS, HEAD_DIM), 0.1),
        "lambda_q2": nrm(ks[17], (N_ATTN_LAYERS, HEAD_DIM), 0.1),
        "lambda_k2": nrm(ks[18], (N_ATTN_LAYERS, HEAD_DIM), 0.1),
        "subln": 1.0 + nrm(ks[19], (N_ATTN_LAYERS, V_HEAD_DIM), 0.02),
        "w_out_attn": nrm(ks[20], (N_ATTN_LAYERS, ATTN_WIDTH, D_MODEL), ATTN_WIDTH ** -0.5),
    }


def reference(x_prompt, x_sample, state_pool, cache_k, cache_v, page_table,
              norm_pool, w_in_pool, w_grp_pool, scale_pool, w_out_pool,
              norm_attn, w_in_attn, q_norm, k_norm, lambda_q1, lambda_k1, lambda_q2, lambda_k2,
              subln, w_out_attn):
    n_dec = x_sample.shape[0]
    pos_prompt = jnp.arange(SEQ)
    pos_sample = PAST_LEN + jnp.arange(DEC_SEQ)
    past_pos = jnp.arange(PAST_LEN)
    xp, xs = x_prompt, x_sample
    ps_prompt, ps_sample, kp, vp, ksm, vsm = [], [], [], [], [], []
    for i in range(DEPTH):
        j = i // N_MIXERS
        if i % N_MIXERS == 0:
            empty = jnp.zeros((xp.shape[0], 0, POOL_WIDTH), xp.dtype)
            xp, st_p = pool_layer(xp, empty, 0, norm_pool[j], w_in_pool[j], w_grp_pool[j], scale_pool[j], w_out_pool[j])
            xs, st_s = pool_layer(xs, state_pool[j], PAST_LEN, norm_pool[j], w_in_pool[j], w_grp_pool[j], scale_pool[j], w_out_pool[j])
            ps_prompt.append(st_p)
            ps_sample.append(st_s)
        else:
            lam_init = 0.8 - 0.6 * math.exp(-0.3 * i)
            wargs = (norm_attn[j], w_in_attn[j], q_norm[j], k_norm[j], lambda_q1[j], lambda_k1[j],
                     lambda_q2[j], lambda_k2[j], subln[j], w_out_attn[j], lam_init)
            xp, k_new, v_new = attn_layer(xp, pos_prompt, None, None, None, *wargs)
            past_k = cache_k[j, page_table].reshape(n_dec, PAST_LEN, N_HEADS, 2, HEAD_DIM)
            past_v = cache_v[j, page_table].reshape(n_dec, PAST_LEN, N_HEADS, V_HEAD_DIM)
            xs, k_s, v_s = attn_layer(xs, pos_sample, past_k, past_v, past_pos, *wargs)
            kp.append(k_new)
            vp.append(v_new)
            ksm.append(k_s)
            vsm.append(v_s)
    return (xp, xs, jnp.stack(ps_prompt), jnp.stack(ps_sample), jnp.stack(kp), jnp.stack(vp), jnp.stack(ksm), jnp.stack(vsm))
```

```python
import functools
import math

import jax
import jax.numpy as jnp
from jax import lax
from jax.experimental import pallas as pl
from jax.experimental.pallas import tpu as pltpu

D_MODEL = 1024
POOL_WIDTH = 2048
POOL_WINDOWS = (2, 4, 8, 16)
POOL_GROUP_WIDTH = 512
POOL_STATE_LEN = 15
N_HEADS = 8
HEAD_DIM = 64
V_HEAD_DIM = 128
PAGE_SIZE = 128
ROPE_THETA = 10000.0
RMS_EPS = 1e-6
NEG_INF = -1e30

VMEM_LIMIT_BYTES = 56 * 1024 * 1024
HALO = 16
TM_PROJ = 256
TQ = 512
TK = TM_PROJ

F32 = jnp.float32
BF16 = jnp.bfloat16


def _params(*sem):
    return pltpu.CompilerParams(dimension_semantics=sem, vmem_limit_bytes=VMEM_LIMIT_BYTES)


def _rms_rows(x, g):
    ms = jnp.mean(x * x, axis=-1, keepdims=True)
    return x * lax.rsqrt(ms + RMS_EPS) * g


def _silu(z):
    return z * jax.nn.sigmoid(z)


def _dot(a, b):
    return jnp.dot(a, b, preferred_element_type=F32)


def _pool_prompt_kernel(x_ref, g_ref, win_ref, wgrp_ref, scale_ref, wout_ref,
                        y_ref, st_ref, ubuf, *, tm):
    t = pl.program_id(1)

    @pl.when(t == 0)
    def _():
        ubuf[0:HALO, :] = jnp.zeros((HALO, POOL_WIDTH), F32)

    x = x_ref[0]
    h = _rms_rows(x, g_ref[...]).astype(BF16)
    uz = _dot(h, win_ref[...])
    u = uz[:, :POOL_WIDTH]
    z = uz[:, POOL_WIDTH:]
    ubuf[HALO:HALO + tm, :] = u

    pos = t * tm + lax.broadcasted_iota(jnp.int32, (tm, 1), 0)
    ms = []
    for gi, w in enumerate(POOL_WINDOWS):
        cols = slice(gi * POOL_GROUP_WIDTH, (gi + 1) * POOL_GROUP_WIDTH)
        acc = ubuf[HALO:HALO + tm, cols]
        for i in range(1, w):
            acc = acc + ubuf[HALO - i:HALO - i + tm, cols]
        inv_cnt = 1.0 / jnp.minimum(pos + 1, w).astype(F32)
        r = acc * inv_cnt - u[:, cols]
        ms.append(_dot(r.astype(BF16), wgrp_ref[gi]))
    m = jnp.concatenate(ms, axis=-1) * scale_ref[...]
    gated = (m * _silu(z)).astype(BF16)
    y_ref[0] = x + _dot(gated, wout_ref[...])

    @pl.when(t == pl.num_programs(1) - 1)
    def _():
        st_ref[0] = ubuf[HALO + tm - POOL_STATE_LEN:HALO + tm, :]

    ubuf[0:HALO, :] = ubuf[tm:tm + HALO, :]


def _pool_prompt(x, g, w_in, w_grp, scale, w_out, tm=TM_PROJ):
    b, s, d = x.shape
    const = lambda *shape: pl.BlockSpec(shape, lambda i, j: (0,) * len(shape))
    return pl.pallas_call(
        functools.partial(_pool_prompt_kernel, tm=tm),
        out_shape=(jax.ShapeDtypeStruct((b, s, d), F32),
                   jax.ShapeDtypeStruct((b, POOL_STATE_LEN, POOL_WIDTH), F32)),
        grid=(b, s // tm),
        in_specs=[
            pl.BlockSpec((1, tm, d), lambda i, j: (i, j, 0)),
            const(1, d),
            const(d, 2 * POOL_WIDTH),
            const(len(POOL_WINDOWS), POOL_GROUP_WIDTH, POOL_GROUP_WIDTH),
            const(1, POOL_WIDTH),
            const(POOL_WIDTH, d),
        ],
        out_specs=(pl.BlockSpec((1, tm, d), lambda i, j: (i, j, 0)),
                   pl.BlockSpec((1, POOL_STATE_LEN, POOL_WIDTH), lambda i, j: (i, 0, 0))),
        scratch_shapes=[pltpu.VMEM((tm + HALO, POOL_WIDTH), F32)],
        compiler_params=_params("arbitrary", "arbitrary"),
        name="pool_prompt",
    )(x, g, w_in, w_grp, scale, w_out)


def _pool_sample_kernel(x_ref, st_ref, g_ref, win_ref, wgrp_ref, scale_ref, wout_ref,
                        y_ref, u_ref):
    x = x_ref[...]
    h = _rms_rows(x, g_ref[...]).astype(BF16)
    uz = _dot(h, win_ref[...])
    u = uz[:, :POOL_WIDTH]
    z = uz[:, POOL_WIDTH:]
    u_ref[...] = u
    row = lax.broadcasted_iota(jnp.int32, (x.shape[0], POOL_STATE_LEN, POOL_GROUP_WIDTH), 1)
    ms = []
    for gi, w in enumerate(POOL_WINDOWS):
        cols = slice(gi * POOL_GROUP_WIDTH, (gi + 1) * POOL_GROUP_WIDTH)
        st = st_ref[:, :, cols]
        tail = jnp.sum(jnp.where(row >= POOL_STATE_LEN - (w - 1), st, 0.0), axis=1)
        r = (tail + u[:, cols]) * (1.0 / w) - u[:, cols]
        ms.append(_dot(r.astype(BF16), wgrp_ref[gi]))
    m = jnp.concatenate(ms, axis=-1) * scale_ref[...]
    gated = (m * _silu(z)).astype(BF16)
    y_ref[...] = x + _dot(gated, wout_ref[...])


def _pool_sample(x, state, g, w_in, w_grp, scale, w_out, tb=32):
    n, d = x.shape
    const = lambda *shape: pl.BlockSpec(shape, lambda i: (0,) * len(shape))
    return pl.pallas_call(
        _pool_sample_kernel,
        out_shape=(jax.ShapeDtypeStruct((n, d), F32),
                   jax.ShapeDtypeStruct((n, POOL_WIDTH), F32)),
        grid=(n // tb,),
        in_specs=[
            pl.BlockSpec((tb, d), lambda i: (i, 0)),
            pl.BlockSpec((tb, POOL_STATE_LEN, POOL_WIDTH), lambda i: (i, 0, 0)),
            const(1, d),
            const(d, 2 * POOL_WIDTH),
            const(len(POOL_WINDOWS), POOL_GROUP_WIDTH, POOL_GROUP_WIDTH),
            const(1, POOL_WIDTH),
            const(POOL_WIDTH, d),
        ],
        out_specs=(pl.BlockSpec((tb, d), lambda i: (i, 0)),
                   pl.BlockSpec((tb, POOL_WIDTH), lambda i: (i, 0))),
        compiler_params=_params("arbitrary"),
        name="pool_sample",
    )(x, state, g, w_in, w_grp, scale, w_out)


def _seg_norm_rope(a, ones_ref, gain, cos, sin):
    sq = a * a
    hi = sq.astype(BF16)
    lo = (sq - hi.astype(F32)).astype(BF16)
    parts = []
    for c in range(a.shape[1] // 256):
        cs = slice(c * 256, (c + 1) * 256)
        parts.append(_dot(hi[:, cs], ones_ref[...]) + _dot(lo[:, cs], ones_ref[...]))
    ss = jnp.concatenate(parts, axis=-1)
    an = a * lax.rsqrt(ss * (1.0 / HEAD_DIM) + RMS_EPS) * gain
    lane = lax.broadcasted_iota(jnp.int32, (a.shape[0], V_HEAD_DIM), 1)
    first_half = (lane % HEAD_DIM) < (HEAD_DIM // 2)
    outs = []
    for hh in range(N_HEADS):
        xh = an[:, hh * V_HEAD_DIM:(hh + 1) * V_HEAD_DIM]
        swapped = jnp.where(first_half,
                            pltpu.roll(xh, V_HEAD_DIM - HEAD_DIM // 2, 1),
                            pltpu.roll(xh, HEAD_DIM // 2, 1))
        outs.append(xh * cos + swapped * sin)
    return jnp.concatenate(outs, axis=-1)


def _attn_proj_kernel(x_ref, g_ref, win_ref, ones_ref, qn_ref, kn_ref, cos_ref, sin_ref,
                      q_ref, k_ref, kb_ref, v_ref, vt_ref, z_ref):
    x = x_ref[...]
    h = _rms_rows(x, g_ref[...]).astype(BF16)
    proj = _dot(h, win_ref[...])
    cos = cos_ref[...]
    sin = sin_ref[...]
    q = _seg_norm_rope(proj[:, 0:D_MODEL], ones_ref, qn_ref[...], cos, sin)
    k = _seg_norm_rope(proj[:, D_MODEL:2 * D_MODEL], ones_ref, kn_ref[...], cos, sin)
    v = proj[:, 2 * D_MODEL:3 * D_MODEL]
    q_ref[...] = (q * (HEAD_DIM ** -0.5)).astype(BF16)
    k_ref[...] = k
    kb_ref[...] = k.astype(BF16)
    v_ref[...] = v
    vt_ref[:, 0] = v.T.astype(BF16).reshape(N_HEADS, V_HEAD_DIM, v.shape[0])
    z_ref[...] = proj[:, 3 * D_MODEL:]


def _attn_proj(x2d, g, w_in, ones, qn_full, kn_full, cos, sin, *, tm, pos_tiles):
    r, d = x2d.shape
    nt = r // tm
    const = lambda *shape: pl.BlockSpec(shape, lambda i: (0,) * len(shape))
    row = lambda width: pl.BlockSpec((tm, width), lambda i: (i, 0))
    tab = pl.BlockSpec((tm, V_HEAD_DIM), lambda i: (i % pos_tiles, 0))
    return pl.pallas_call(
        _attn_proj_kernel,
        out_shape=(jax.ShapeDtypeStruct((r, d), BF16),
                   jax.ShapeDtypeStruct((r, d), F32),
                   jax.ShapeDtypeStruct((r, d), BF16),
                   jax.ShapeDtypeStruct((r, d), F32),
                   jax.ShapeDtypeStruct((N_HEADS, nt, V_HEAD_DIM, tm), BF16),
                   jax.ShapeDtypeStruct((r, d), F32)),
        grid=(nt,),
        in_specs=[row(d), const(1, d), const(d, 4 * d), const(256, 256),
                  const(1, d), const(1, d), tab, tab],
        out_specs=(row(d), row(d), row(d), row(d),
                   pl.BlockSpec((N_HEADS, 1, V_HEAD_DIM, tm), lambda i: (0, i, 0, 0)),
                   row(d)),
        compiler_params=_params("arbitrary"),
        name="attn_proj",
    )(x2d, g, w_in, ones, qn_full, kn_full, cos, sin)


def _lambda(lq1, lk1, lq2, lk2, lam_init):
    a = jnp.sum(lq1 * lk1, axis=-1, keepdims=True)
    b = jnp.sum(lq2 * lk2, axis=-1, keepdims=True)
    return jnp.exp(a) - jnp.exp(b) + lam_init


def _flash_kernel(lq1_ref, lk1_ref, lq2_ref, lk2_ref, subg_ref, q_ref, k_ref, vt_ref,
                  o_ref, qc_s, m_s, l_s, acc_s, *, lam_init):
    qi = pl.program_id(2)
    q = q_ref[...]
    lane = lax.broadcasted_iota(jnp.int32, q.shape, 1)
    zero = jnp.zeros_like(q)
    qc_s[0] = jnp.where(lane < HEAD_DIM, q, zero)
    qc_s[1] = jnp.where(lane >= HEAD_DIM, q, zero)
    m_s[...] = jnp.full(m_s.shape, NEG_INF, F32)
    l_s[...] = jnp.zeros(l_s.shape, F32)
    acc_s[...] = jnp.zeros(acc_s.shape, F32)

    def step(ki, masked):
        k = k_ref[pl.ds(pl.multiple_of(ki * TK, TK), TK), :]
        vt = vt_ref[0, ki]
        if masked:
            kpos = ki * TK + lax.broadcasted_iota(jnp.int32, (TK, TQ), 0)
            qpos = qi * TQ + lax.broadcasted_iota(jnp.int32, (TK, TQ), 1)
            keep = kpos <= qpos
        for c in range(2):
            s = lax.dot_general(k, qc_s[c], (((1,), (1,)), ((), ())),
                                preferred_element_type=F32)
            if masked:
                s = jnp.where(keep, s, NEG_INF)
            m_old = m_s[c]
            m_new = jnp.maximum(m_old, jnp.max(s, axis=0, keepdims=True))
            alpha = jnp.exp(m_old - m_new)
            p = jnp.exp(s - m_new)
            l_s[c] = alpha * l_s[c] + jnp.sum(p, axis=0, keepdims=True)
            acc_s[c] = alpha * acc_s[c] + _dot(vt, p.astype(BF16))
            m_s[c] = m_new

    def body(ki, carry):
        step(ki, False)
        return carry

    n_full = qi * (TQ // TK)
    lax.fori_loop(0, n_full, body, 0)
    for j in range(TQ // TK):
        step(n_full + j, True)

    lam = _lambda(lq1_ref[...], lk1_ref[...], lq2_ref[...], lk2_ref[...], lam_init)
    ot = acc_s[0] / l_s[0] - lam * (acc_s[1] / l_s[1])
    o = ot.T
    o_ref[...] = _rms_rows(o, subg_ref[...]) * (1.0 - lam_init)


def _flash_prompt(q, kb, vt, lq1, lk1, lq2, lk2, subg, *, batch, seq, lam_init):
    nq = seq // TQ
    nk = seq // TK
    vec = lambda n: pl.BlockSpec((1, n), lambda b, h, i: (0, 0))
    return pl.pallas_call(
        functools.partial(_flash_kernel, lam_init=lam_init),
        out_shape=jax.ShapeDtypeStruct((batch * seq, N_HEADS * V_HEAD_DIM), F32),
        grid=(batch, N_HEADS, nq),
        in_specs=[vec(HEAD_DIM), vec(HEAD_DIM), vec(HEAD_DIM), vec(HEAD_DIM), vec(V_HEAD_DIM),
                  pl.BlockSpec((TQ, V_HEAD_DIM), lambda b, h, i: (b * nq + i, h)),
                  pl.BlockSpec((seq, V_HEAD_DIM), lambda b, h, i: (b, h)),
                  pl.BlockSpec((1, nk, V_HEAD_DIM, TK), lambda b, h, i: (h, b, 0, 0))],
        out_specs=pl.BlockSpec((TQ, V_HEAD_DIM), lambda b, h, i: (b * nq + i, h)),
        scratch_shapes=[pltpu.VMEM((2, TQ, V_HEAD_DIM), BF16),
                        pltpu.VMEM((2, 1, TQ), F32),
                        pltpu.VMEM((2, 1, TQ), F32),
                        pltpu.VMEM((2, V_HEAD_DIM, TQ), F32)],
        compiler_params=_params("arbitrary", "arbitrary", "arbitrary"),
        name="flash_prompt",
    )(lq1, lk1, lq2, lk2, subg, q, kb, vt)


def _decode_kernel(pt_ref, lq1_ref, lk1_ref, lq2_ref, lk2_ref, subg_ref,
                   q_ref, kn_ref, vn_ref, *rest, n_pages, lam_init):
    k_refs = rest[:n_pages]
    v_refs = rest[n_pages:2 * n_pages]
    o_ref = rest[2 * n_pages]
    kall, vall = rest[2 * n_pages + 1:]
    del pt_ref

    nseg = 2 * N_HEADS
    q = q_ref[0].astype(F32)
    seg = lax.broadcasted_iota(jnp.int32, (nseg, D_MODEL), 1) // HEAD_DIM
    rowi = lax.broadcasted_iota(jnp.int32, (nseg, D_MODEL), 0)
    row_seg = 2 * (rowi % N_HEADS) + rowi // N_HEADS
    qbd = jnp.where(seg == row_seg, jnp.broadcast_to(q, (nseg, D_MODEL)), 0.0)

    for i in range(n_pages):
        kall[i * PAGE_SIZE:(i + 1) * PAGE_SIZE, :] = k_refs[i][0].astype(BF16)
        vall[i * PAGE_SIZE:(i + 1) * PAGE_SIZE, :] = v_refs[i][0].astype(BF16)

    s = lax.dot_general(qbd.astype(BF16), kall[...], (((1,), (1,)), ((), ())),
                        preferred_element_type=F32)
    kn = kn_ref[0].astype(BF16).astype(F32)
    s_new = jnp.sum(qbd * kn, axis=-1, keepdims=True)
    m = jnp.maximum(jnp.max(s, axis=-1, keepdims=True), s_new)
    p = jnp.exp(s - m)
    p_new = jnp.exp(s_new - m)
    inv_l = 1.0 / (jnp.sum(p, axis=-1, keepdims=True) + p_new)
    p = p * inv_l
    p_new = p_new * inv_l

    lam = _lambda(lq1_ref[...], lk1_ref[...], lq2_ref[...], lk2_ref[...], lam_init)
    a = p[:N_HEADS] - lam * p[N_HEADS:]
    a_new = p_new[:N_HEADS] - lam * p_new[N_HEADS:]

    vn = vn_ref[0].astype(BF16).astype(F32)
    o = _dot(a.astype(BF16), vall[...]) + a_new.astype(BF16).astype(F32) * vn
    head_of_lane = lax.broadcasted_iota(jnp.int32, (N_HEADS, D_MODEL), 1) // V_HEAD_DIM
    head_of_row = lax.broadcasted_iota(jnp.int32, (N_HEADS, D_MODEL), 0)
    o = jnp.where(head_of_lane == head_of_row, o, 0.0)
    ms = jnp.sum(o * o, axis=-1, keepdims=True) * (1.0 / V_HEAD_DIM)
    o = o * lax.rsqrt(ms + RMS_EPS)
    o_ref[0] = jnp.sum(o, axis=0, keepdims=True) * subg_ref[...] * (1.0 - lam_init)


def _decode_attn(page_table, q, k_new, v_new, cache_k, cache_v, lq1, lk1, lq2, lk2, subg_full,
                 *, lam_init):
    n, n_pages = page_table.shape
    vec = lambda w: pl.BlockSpec((1, w), lambda b, pt: (0, 0))
    row = pl.BlockSpec((1, 1, D_MODEL), lambda b, pt: (b, 0, 0))

    def page_spec(i):
        return pl.BlockSpec((1, PAGE_SIZE, D_MODEL), lambda b, pt: (pt[b, i], 0, 0))

    in_specs = ([vec(HEAD_DIM)] * 4 + [vec(D_MODEL), row, row, row]
                + [page_spec(i) for i in range(n_pages)] * 2)
    past = n_pages * PAGE_SIZE
    out = pl.pallas_call(
        functools.partial(_decode_kernel, n_pages=n_pages, lam_init=lam_init),
        out_shape=jax.ShapeDtypeStruct((n, 1, D_MODEL), F32),
        grid_spec=pltpu.PrefetchScalarGridSpec(
            num_scalar_prefetch=1,
            grid=(n,),
            in_specs=in_specs,
            out_specs=row,
            scratch_shapes=[pltpu.VMEM((past, D_MODEL), BF16),
                            pltpu.VMEM((past, D_MODEL), BF16)],
        ),
        compiler_params=_params("arbitrary"),
        name="decode_attn",
    )(page_table, lq1, lk1, lq2, lk2, subg_full,
      q.reshape(n, 1, D_MODEL), k_new.reshape(n, 1, D_MODEL), v_new.reshape(n, 1, D_MODEL),
      *([cache_k] * n_pages), *([cache_v] * n_pages))
    return out.reshape(n, D_MODEL)


def _attn_out_kernel(x_ref, o_ref, z_ref, w_ref, y_ref):
    gated = (o_ref[...] * _silu(z_ref[...])).astype(BF16)
    y_ref[...] = x_ref[...] + _dot(gated, w_ref[...])


def _attn_out(x2d, o, z, w_out, *, tm):
    r, d = x2d.shape
    row = pl.BlockSpec((tm, d), lambda i: (i, 0))
    return pl.pallas_call(
        _attn_out_kernel,
        out_shape=jax.ShapeDtypeStruct((r, d), F32),
        grid=(r // tm,),
        in_specs=[row, row, row, pl.BlockSpec((d, d), lambda i: (0, 0))],
        out_specs=row,
        compiler_params=_params("arbitrary"),
        name="attn_out",
    )(x2d, o, z, w_out)


def _rope_tables(pos):
    half = HEAD_DIM // 2
    inv = jnp.power(ROPE_THETA, -jnp.arange(0, HEAD_DIM, 2, dtype=F32) / HEAD_DIM)
    ang = pos.astype(F32)[:, None] * inv[None, :]
    cos = jnp.cos(ang)
    sin = jnp.sin(ang)
    cos_full = jnp.tile(cos, (1, V_HEAD_DIM // half))
    sin_full = jnp.tile(jnp.concatenate([-sin, sin], axis=-1), (1, V_HEAD_DIM // HEAD_DIM))
    return cos_full, sin_full


def kernel(x_prompt, x_sample, state_pool, cache_k, cache_v, page_table, norm_pool, w_in_pool, w_grp_pool, scale_pool, w_out_pool, norm_attn, w_in_attn, q_norm, k_norm, lambda_q1, lambda_k1, lambda_q2, lambda_k2, subln, w_out_attn):
    batch, seq, d = x_prompt.shape
    n_dec = x_sample.shape[0]
    depth = norm_pool.shape[0] + norm_attn.shape[0]
    past_len = page_table.shape[1] * PAGE_SIZE
    n_phys = cache_k.shape[1]

    cos_p, sin_p = _rope_tables(jnp.arange(seq))
    cos_s, sin_s = _rope_tables(jnp.full((n_dec,), past_len, jnp.int32))
    seg = jnp.arange(256) // HEAD_DIM
    ones_bd = (seg[:, None] == seg[None, :]).astype(BF16)

    xp = x_prompt
    xs = x_sample.reshape(n_dec, d)
    ps_prompt, ps_sample, kp, vp, ksm, vsm = [], [], [], [], [], []
    for i in range(depth):
        j = i // 2
        if i % 2 == 0:
            g = norm_pool[j].reshape(1, d)
            w_in = w_in_pool[j].astype(BF16)
            w_grp = w_grp_pool[j].astype(BF16)
            scale = scale_pool[j].reshape(1, POOL_WIDTH)
            w_out = w_out_pool[j].astype(BF16)
            xp, st_p = _pool_prompt(xp, g, w_in, w_grp, scale, w_out)
            xs, u_s = _pool_sample(xs, state_pool[j], g, w_in, w_grp, scale, w_out)
            ps_prompt.append(st_p)
            ps_sample.append(jnp.concatenate([state_pool[j][:, 1:], u_s[:, None, :]], axis=1))
        else:
            lam_init = 0.8 - 0.6 * math.exp(-0.3 * i)
            g = norm_attn[j].reshape(1, d)
            w_in = w_in_attn[j].astype(BF16)
            w_out = w_out_attn[j].astype(BF16)
            qn = jnp.tile(q_norm[j], d // HEAD_DIM).reshape(1, d)
            kn = jnp.tile(k_norm[j], d // HEAD_DIM).reshape(1, d)
            lams = [a[j].reshape(1, HEAD_DIM) for a in (lambda_q1, lambda_k1, lambda_q2, lambda_k2)]
            subg = subln[j].reshape(1, V_HEAD_DIM)

            x2 = xp.reshape(batch * seq, d)
            q, k, kb, v, vt, z = _attn_proj(x2, g, w_in, ones_bd, qn, kn, cos_p, sin_p,
                                            tm=TM_PROJ, pos_tiles=seq // TM_PROJ)
            o = _flash_prompt(q, kb, vt, *lams, subg, batch=batch, seq=seq, lam_init=lam_init)
            xp = _attn_out(x2, o, z, w_out, tm=512).reshape(batch, seq, d)
            kp.append(k.reshape(batch, seq, N_HEADS, 2, HEAD_DIM))
            vp.append(v.reshape(batch, seq, N_HEADS, V_HEAD_DIM))

            qs, k_s, _, v_s, _, z_s = _attn_proj(xs, g, w_in, ones_bd, qn, kn, cos_s, sin_s,
                                                 tm=n_dec, pos_tiles=1)
            o_s = _decode_attn(page_table, qs, k_s, v_s,
                               cache_k[j].reshape(n_phys, PAGE_SIZE, d),
                               cache_v[j].reshape(n_phys, PAGE_SIZE, d),
                               *lams, jnp.tile(subg, (1, N_HEADS)), lam_init=lam_init)
            xs = _attn_out(xs, o_s, z_s, w_out, tm=n_dec)
            ksm.append(k_s.reshape(n_dec, 1, N_HEADS, 2, HEAD_DIM))
            vsm.append(v_s.reshape(n_dec, 1, N_HEADS, V_HEAD_DIM))
    return (xp, xs.reshape(n_dec, 1, d), jnp.stack(ps_prompt), jnp.stack(ps_sample),
            jnp.stack(kp), jnp.stack(vp), jnp.stack(ksm), jnp.stack(vsm))
```

```python
import functools
import math

import jax
import jax.numpy as jnp
from jax import lax
from jax.experimental import pallas as pl
from jax.experimental.pallas import tpu as pltpu

D_MODEL = 1024
POOL_WIDTH = 2048
POOL_WINDOWS = (2, 4, 8, 16)
POOL_GROUP_WIDTH = 512
POOL_STATE_LEN = 15
N_HEADS = 8
HEAD_DIM = 64
V_HEAD_DIM = 128
PAGE_SIZE = 128
ROPE_THETA = 10000.0
RMS_EPS = 1e-6
NEG_INF = -1e30

VMEM_LIMIT_BYTES = 56 * 1024 * 1024
HALO = 16
TM_PROJ = 256
TQ = 512
TK = TM_PROJ

F32 = jnp.float32
BF16 = jnp.bfloat16


def _params(*sem):
    return pltpu.CompilerParams(dimension_semantics=sem, vmem_limit_bytes=VMEM_LIMIT_BYTES)


def _rms_rows(x, g):
    ms = jnp.mean(x * x, axis=-1, keepdims=True)
    return x * lax.rsqrt(ms + RMS_EPS) * g


def _silu(z):
    return z * jax.nn.sigmoid(z)


def _dot(a, b):
    return jnp.dot(a, b, preferred_element_type=F32)


def _pool_prompt_kernel(x_ref, g_ref, win_ref, wgrp_ref, scale_ref, wout_ref,
                        y_ref, st_ref, ubuf, *, tm):
    t = pl.program_id(1)

    @pl.when(t == 0)
    def _():
        ubuf[0:HALO, :] = jnp.zeros((HALO, POOL_WIDTH), F32)

    x = x_ref[0]
    h = _rms_rows(x, g_ref[...]).astype(BF16)
    uz = _dot(h, win_ref[...])
    u = uz[:, :POOL_WIDTH]
    z = uz[:, POOL_WIDTH:]
    ubuf[HALO:HALO + tm, :] = u

    pos = t * tm + lax.broadcasted_iota(jnp.int32, (tm, 1), 0)
    ms = []
    for gi, w in enumerate(POOL_WINDOWS):
        cols = slice(gi * POOL_GROUP_WIDTH, (gi + 1) * POOL_GROUP_WIDTH)
        acc = ubuf[HALO:HALO + tm, cols]
        for i in range(1, w):
            acc = acc + ubuf[HALO - i:HALO - i + tm, cols]
        inv_cnt = 1.0 / jnp.minimum(pos + 1, w).astype(F32)
        r = acc * inv_cnt - u[:, cols]
        ms.append(_dot(r.astype(BF16), wgrp_ref[gi]))
    m = jnp.concatenate(ms, axis=-1) * scale_ref[...]
    gated = (m * _silu(z)).astype(BF16)
    y_ref[0] = x + _dot(gated, wout_ref[...])

    @pl.when(t == pl.num_programs(1) - 1)
    def _():
        st_ref[0] = ubuf[HALO + tm - POOL_STATE_LEN:HALO + tm, :]

    ubuf[0:HALO, :] = ubuf[tm:tm + HALO, :]


def _pool_prompt(x, g, w_in, w_grp, scale, w_out, tm=TM_PROJ):
    b, s, d = x.shape
    const = lambda *shape: pl.BlockSpec(shape, lambda i, j: (0,) * len(shape))
    return pl.pallas_call(
        functools.partial(_pool_prompt_kernel, tm=tm),
        out_shape=(jax.ShapeDtypeStruct((b, s, d), F32),
                   jax.ShapeDtypeStruct((b, POOL_STATE_LEN, POOL_WIDTH), F32)),
        grid=(b, s // tm),
        in_specs=[
            pl.BlockSpec((1, tm, d), lambda i, j: (i, j, 0)),
            const(1, d),
            const(d, 2 * POOL_WIDTH),
            const(len(POOL_WINDOWS), POOL_GROUP_WIDTH, POOL_GROUP_WIDTH),
            const(1, POOL_WIDTH),
            const(POOL_WIDTH, d),
        ],
        out_specs=(pl.BlockSpec((1, tm, d), lambda i, j: (i, j, 0)),
                   pl.BlockSpec((1, POOL_STATE_LEN, POOL_WIDTH), lambda i, j: (i, 0, 0))),
        scratch_shapes=[pltpu.VMEM((tm + HALO, POOL_WIDTH), F32)],
        compiler_params=_params("arbitrary", "arbitrary"),
        name="pool_prompt",
    )(x, g, w_in, w_grp, scale, w_out)


def _pool_sample_kernel(x_ref, st_ref, g_ref, win_ref, wgrp_ref, scale_ref, wout_ref,
                        y_ref, nst_ref):
    x = x_ref[...]
    h = _rms_rows(x, g_ref[...]).astype(BF16)
    uz = _dot(h, win_ref[...])
    u = uz[:, :POOL_WIDTH]
    z = uz[:, POOL_WIDTH:]
    nst_ref[0, 0:POOL_STATE_LEN - 1] = st_ref[0, 1:POOL_STATE_LEN]
    nst_ref[0, POOL_STATE_LEN - 1] = u
    ms = []
    for gi, w in enumerate(POOL_WINDOWS):
        cols = slice(gi * POOL_GROUP_WIDTH, (gi + 1) * POOL_GROUP_WIDTH)
        acc = u[:, cols]
        for i in range(1, w):
            acc = acc + st_ref[0, POOL_STATE_LEN - i, :, cols]
        r = acc * (1.0 / w) - u[:, cols]
        ms.append(_dot(r.astype(BF16), wgrp_ref[gi]))
    m = jnp.concatenate(ms, axis=-1) * scale_ref[...]
    gated = (m * _silu(z)).astype(BF16)
    y_ref[...] = x + _dot(gated, wout_ref[...])


def _pool_sample(x, state_t, layer, g, w_in, w_grp, scale, w_out, tb=32):
    n, d = x.shape
    const = lambda *shape: pl.BlockSpec(shape, lambda i: (0,) * len(shape))
    return pl.pallas_call(
        _pool_sample_kernel,
        out_shape=(jax.ShapeDtypeStruct((n, d), F32),
                   jax.ShapeDtypeStruct((1, POOL_STATE_LEN, n, POOL_WIDTH), F32)),
        grid=(n // tb,),
        in_specs=[
            pl.BlockSpec((tb, d), lambda i: (i, 0)),
            pl.BlockSpec((1, POOL_STATE_LEN, tb, POOL_WIDTH), lambda i: (layer, 0, i, 0)),
            const(1, d),
            const(d, 2 * POOL_WIDTH),
            const(len(POOL_WINDOWS), POOL_GROUP_WIDTH, POOL_GROUP_WIDTH),
            const(1, POOL_WIDTH),
            const(POOL_WIDTH, d),
        ],
        out_specs=(pl.BlockSpec((tb, d), lambda i: (i, 0)),
                   pl.BlockSpec((1, POOL_STATE_LEN, tb, POOL_WIDTH), lambda i: (0, 0, i, 0))),
        compiler_params=_params("arbitrary"),
        name="pool_sample",
    )(x, state_t, g, w_in, w_grp, scale, w_out)


def _seg_norm_rope(a, ones_ref, gain, cos, sin):
    sq = a * a
    hi = sq.astype(BF16)
    lo = (sq - hi.astype(F32)).astype(BF16)
    parts = []
    for c in range(a.shape[1] // 256):
        cs = slice(c * 256, (c + 1) * 256)
        parts.append(_dot(hi[:, cs], ones_ref[...]) + _dot(lo[:, cs], ones_ref[...]))
    ss = jnp.concatenate(parts, axis=-1)
    an = a * lax.rsqrt(ss * (1.0 / HEAD_DIM) + RMS_EPS) * gain
    lane = lax.broadcasted_iota(jnp.int32, (a.shape[0], V_HEAD_DIM), 1)
    first_half = (lane % HEAD_DIM) < (HEAD_DIM // 2)
    outs = []
    for hh in range(N_HEADS):
        xh = an[:, hh * V_HEAD_DIM:(hh + 1) * V_HEAD_DIM]
        swapped = jnp.where(first_half,
                            pltpu.roll(xh, V_HEAD_DIM - HEAD_DIM // 2, 1),
                            pltpu.roll(xh, HEAD_DIM // 2, 1))
        outs.append(xh * cos + swapped * sin)
    return jnp.concatenate(outs, axis=-1)


def _attn_proj_kernel(x_ref, g_ref, win_ref, ones_ref, qn_ref, kn_ref, cos_ref, sin_ref,
                      q_ref, kb_ref, kt_ref, v_ref, vt_ref, z_ref):
    x = x_ref[...]
    h = _rms_rows(x, g_ref[...]).astype(BF16)
    proj = _dot(h, win_ref[...])
    cos = cos_ref[...]
    sin = sin_ref[...]
    q = _seg_norm_rope(proj[:, 0:D_MODEL], ones_ref, qn_ref[...], cos, sin)
    k = _seg_norm_rope(proj[:, D_MODEL:2 * D_MODEL], ones_ref, kn_ref[...], cos, sin)
    v = proj[:, 2 * D_MODEL:3 * D_MODEL]
    q_ref[...] = (q * (HEAD_DIM ** -0.5 * math.log2(math.e))).astype(BF16)
    kb_ref[...] = k.astype(BF16)
    kt_ref[0] = k.T
    v_ref[...] = v.reshape(v.shape[0], N_HEADS, V_HEAD_DIM)
    vt_ref[:, 0] = v.T.astype(BF16).reshape(N_HEADS, V_HEAD_DIM, v.shape[0])
    z_ref[...] = proj[:, 3 * D_MODEL:]


def _attn_proj(x2d, g, w_in, ones, qn_full, kn_full, cos, sin, *, tm, pos_tiles):
    r, d = x2d.shape
    nt = r // tm
    const = lambda *shape: pl.BlockSpec(shape, lambda i: (0,) * len(shape))
    row = lambda width: pl.BlockSpec((tm, width), lambda i: (i, 0))
    tab = pl.BlockSpec((tm, V_HEAD_DIM), lambda i: (i % pos_tiles, 0))
    return pl.pallas_call(
        _attn_proj_kernel,
        out_shape=(jax.ShapeDtypeStruct((r, d), BF16),
                   jax.ShapeDtypeStruct((r, d), BF16),
                   jax.ShapeDtypeStruct((nt // pos_tiles, d, pos_tiles * tm), F32),
                   jax.ShapeDtypeStruct((r, N_HEADS, V_HEAD_DIM), F32),
                   jax.ShapeDtypeStruct((N_HEADS, nt, V_HEAD_DIM, tm), BF16),
                   jax.ShapeDtypeStruct((r, d), F32)),
        grid=(nt,),
        in_specs=[row(d), const(1, d), const(d, 4 * d), const(256, 256),
                  const(1, d), const(1, d), tab, tab],
        out_specs=(row(d), row(d),
                   pl.BlockSpec((1, d, tm), lambda i: (i // pos_tiles, 0, i % pos_tiles)),
                   pl.BlockSpec((tm, N_HEADS, V_HEAD_DIM), lambda i: (i, 0, 0)),
                   pl.BlockSpec((N_HEADS, 1, V_HEAD_DIM, tm), lambda i: (0, i, 0, 0)),
                   row(d)),
        compiler_params=_params("arbitrary"),
        name="attn_proj",
    )(x2d, g, w_in, ones, qn_full, kn_full, cos, sin)


def _lambda(lq1, lk1, lq2, lk2, lam_init):
    a = jnp.sum(lq1 * lk1, axis=-1, keepdims=True)
    b = jnp.sum(lq2 * lk2, axis=-1, keepdims=True)
    return jnp.exp(a) - jnp.exp(b) + lam_init


def _flash_kernel(lq1_ref, lk1_ref, lq2_ref, lk2_ref, subg_ref, q_ref, k_ref, vt_ref,
                  o_ref, qc_s, s_a, s_b, acc_s, *, lam_init):
    qi = pl.program_id(2)
    q = q_ref[...]
    lane = lax.broadcasted_iota(jnp.int32, q.shape, 1)
    zero = jnp.zeros_like(q)
    qc_s[0] = jnp.where(lane < HEAD_DIM, q, zero)
    qc_s[1] = jnp.where(lane >= HEAD_DIM, q, zero)
    acc_s[...] = jnp.zeros(acc_s.shape, F32)

    def scores(ki, s_ref):
        k = k_ref[pl.ds(pl.multiple_of(ki * TK, TK), TK), :]
        for c in range(2):
            s_ref[c] = lax.dot_general(k, qc_s[c], (((1,), (1,)), ((), ())),
                                       preferred_element_type=F32)

    def softmax_pv(ki, s_ref, carry, masked):
        vt = vt_ref[0, ki]
        if masked:
            kpos = ki * TK + lax.broadcasted_iota(jnp.int32, (TK, TQ), 0)
            qpos = qi * TQ + lax.broadcasted_iota(jnp.int32, (TK, TQ), 1)
            keep = kpos <= qpos
        out = []
        for c in range(2):
            m_old, l_old = carry[c]
            s = s_ref[c]
            if masked:
                s = jnp.where(keep, s, NEG_INF)
            m_new = jnp.maximum(m_old, jnp.max(s, axis=0, keepdims=True))
            alpha = jnp.exp2(m_old - m_new)
            p = jnp.exp2(s - m_new)
            l_new = alpha * l_old + jnp.sum(p, axis=0, keepdims=True)
            acc_s[c] = alpha * acc_s[c] + _dot(vt, p.astype(BF16))
            out.append((m_new, l_new))
        return tuple(out)

    def body(kk, carry):
        k0 = 2 * kk
        scores(k0 + 1, s_b)
        carry = softmax_pv(k0, s_a, carry, False)
        scores(k0 + 2, s_a)
        return softmax_pv(k0 + 1, s_b, carry, False)

    neg = jnp.full((1, TQ), NEG_INF, F32)
    zero_l = jnp.zeros((1, TQ), F32)
    scores(0, s_a)
    carry = lax.fori_loop(0, qi, body, ((neg, zero_l), (neg, zero_l)))
    scores(2 * qi + 1, s_b)
    carry = softmax_pv(2 * qi, s_a, carry, True)
    carry = softmax_pv(2 * qi + 1, s_b, carry, True)

    lam = _lambda(lq1_ref[...], lk1_ref[...], lq2_ref[...], lk2_ref[...], lam_init)
    ot = acc_s[0] / carry[0][1] - lam * (acc_s[1] / carry[1][1])
    o = ot.T
    o_ref[...] = _rms_rows(o, subg_ref[...]) * (1.0 - lam_init)


def _flash_prompt(q, kb, vt, lq1, lk1, lq2, lk2, subg, *, batch, seq, lam_init):
    nq = seq // TQ
    nk = seq // TK
    vec = lambda n: pl.BlockSpec((1, n), lambda b, h, i: (0, 0))
    return pl.pallas_call(
        functools.partial(_flash_kernel, lam_init=lam_init),
        out_shape=jax.ShapeDtypeStruct((batch * seq, N_HEADS * V_HEAD_DIM), F32),
        grid=(batch, N_HEADS, nq),
        in_specs=[vec(HEAD_DIM), vec(HEAD_DIM), vec(HEAD_DIM), vec(HEAD_DIM), vec(V_HEAD_DIM),
                  pl.BlockSpec((TQ, V_HEAD_DIM), lambda b, h, i: (b * nq + i, h)),
                  pl.BlockSpec((seq, V_HEAD_DIM), lambda b, h, i: (b, h)),
                  pl.BlockSpec((1, nk, V_HEAD_DIM, TK), lambda b, h, i: (h, b, 0, 0))],
        out_specs=pl.BlockSpec((TQ, V_HEAD_DIM), lambda b, h, i: (b * nq + i, h)),
        scratch_shapes=[pltpu.VMEM((2, TQ, V_HEAD_DIM), BF16),
                        pltpu.VMEM((2, TK, TQ), F32),
                        pltpu.VMEM((2, TK, TQ), F32),
                        pltpu.VMEM((2, V_HEAD_DIM, TQ), F32)],
        compiler_params=_params("arbitrary", "arbitrary", "arbitrary"),
        name="flash_prompt",
    )(lq1, lk1, lq2, lk2, subg, q, kb, vt)


def _decode_kernel(pt_ref, lq1_ref, lk1_ref, lq2_ref, lk2_ref, subg_ref,
                   q_ref, kn_ref, vn_ref, expand_ref, *rest, n_pages, lam_init):
    k_refs = rest[:n_pages]
    v_refs = rest[n_pages:2 * n_pages]
    o_ref = rest[2 * n_pages]
    kall, vall = rest[2 * n_pages + 1:]
    del pt_ref

    nseg = 2 * N_HEADS
    q = q_ref[0].astype(F32)
    seg = lax.broadcasted_iota(jnp.int32, (nseg, D_MODEL), 1) // HEAD_DIM
    rowi = lax.broadcasted_iota(jnp.int32, (nseg, D_MODEL), 0)
    row_seg = 2 * (rowi % N_HEADS) + rowi // N_HEADS
    qbd = jnp.where(seg == row_seg, jnp.broadcast_to(q, (nseg, D_MODEL)), 0.0)

    for i in range(n_pages):
        kall[:, i * PAGE_SIZE:(i + 1) * PAGE_SIZE] = k_refs[i][0, 0].astype(BF16)
        vall[i * D_MODEL:(i + 1) * D_MODEL, :] = v_refs[i][0, 0].astype(BF16)

    s = _dot(qbd.astype(BF16), kall[...])
    kn = kn_ref[0].astype(F32)
    s_new = jnp.sum(qbd * kn, axis=-1, keepdims=True)
    m = jnp.maximum(jnp.max(s, axis=-1, keepdims=True), s_new)
    p = jnp.exp2(s - m)
    p_new = jnp.exp2(s_new - m)
    inv_l = 1.0 / (jnp.sum(p, axis=-1, keepdims=True) + p_new)
    p = p * inv_l
    p_new = p_new * inv_l

    lam = _lambda(lq1_ref[...], lk1_ref[...], lq2_ref[...], lk2_ref[...], lam_init)
    a = p[:N_HEADS] - lam * p[N_HEADS:]
    a_new = p_new[:N_HEADS] - lam * p_new[N_HEADS:]

    a_rows = jnp.concatenate([a[:, i * PAGE_SIZE:(i + 1) * PAGE_SIZE] for i in range(n_pages)],
                             axis=0).astype(BF16)
    w = _dot(a_rows, expand_ref[...])
    same_head = (lax.broadcasted_iota(jnp.int32, w.shape, 0) % N_HEADS
                 == lax.broadcasted_iota(jnp.int32, w.shape, 1) % N_HEADS)
    w = jnp.where(same_head, w, 0.0).astype(BF16)
    w_cat = jnp.concatenate([w[i * N_HEADS:(i + 1) * N_HEADS] for i in range(n_pages)],
                            axis=1)
    vn = vn_ref[0].astype(BF16).astype(F32)
    o = _dot(w_cat, vall[...]) + a_new.astype(BF16).astype(F32) * vn
    o_ref[0] = _rms_rows(o, subg_ref[...]) * (1.0 - lam_init)


def _decode_attn(page_table, q, k_new, v_new, cache_kt, cache_vf, layer, lq1, lk1, lq2, lk2, subg,
                 *, lam_init):
    n, n_pages = page_table.shape
    vec = lambda w: pl.BlockSpec((1, w), lambda b, pt: (0, 0))
    row = pl.BlockSpec((1, 1, D_MODEL), lambda b, pt: (b, 0, 0))
    per_head = pl.BlockSpec((1, N_HEADS, V_HEAD_DIM), lambda b, pt: (b, 0, 0))

    def page_spec(i):
        return pl.BlockSpec((1, 1, D_MODEL, PAGE_SIZE), lambda b, pt: (layer, pt[b, i], 0, 0))

    key = jnp.arange(PAGE_SIZE)
    expand = (key[:, None] == (jnp.arange(D_MODEL) // N_HEADS)[None, :]).astype(BF16)
    in_specs = ([vec(HEAD_DIM)] * 4 + [vec(V_HEAD_DIM), row, row, per_head,
                                       pl.BlockSpec((PAGE_SIZE, D_MODEL), lambda b, pt: (0, 0))]
                + [page_spec(i) for i in range(n_pages)] * 2)
    past = n_pages * PAGE_SIZE
    out = pl.pallas_call(
        functools.partial(_decode_kernel, n_pages=n_pages, lam_init=lam_init),
        out_shape=jax.ShapeDtypeStruct((n, N_HEADS, V_HEAD_DIM), F32),
        grid_spec=pltpu.PrefetchScalarGridSpec(
            num_scalar_prefetch=1,
            grid=(n,),
            in_specs=in_specs,
            out_specs=per_head,
            scratch_shapes=[pltpu.VMEM((D_MODEL, past), BF16),
                            pltpu.VMEM((past * N_HEADS, V_HEAD_DIM), BF16)],
        ),
        compiler_params=_params("arbitrary"),
        name="decode_attn",
    )(page_table, lq1, lk1, lq2, lk2, subg,
      q.reshape(n, 1, D_MODEL), k_new.reshape(n, 1, D_MODEL), v_new, expand,
      *([cache_kt] * n_pages), *([cache_vf] * n_pages))
    return out.reshape(n, D_MODEL)


def _attn_out_kernel(x_ref, o_ref, z_ref, w_ref, y_ref):
    gated = (o_ref[...] * _silu(z_ref[...])).astype(BF16)
    y_ref[...] = x_ref[...] + _dot(gated, w_ref[...])


def _attn_out(x2d, o, z, w_out, *, tm):
    r, d = x2d.shape
    row = pl.BlockSpec((tm, d), lambda i: (i, 0))
    return pl.pallas_call(
        _attn_out_kernel,
        out_shape=jax.ShapeDtypeStruct((r, d), F32),
        grid=(r // tm,),
        in_specs=[row, row, row, pl.BlockSpec((d, d), lambda i: (0, 0))],
        out_specs=row,
        compiler_params=_params("arbitrary"),
        name="attn_out",
    )(x2d, o, z, w_out)


def _rope_tables(pos):
    half = HEAD_DIM // 2
    inv = jnp.power(ROPE_THETA, -jnp.arange(0, HEAD_DIM, 2, dtype=F32) / HEAD_DIM)
    ang = pos.astype(F32)[:, None] * inv[None, :]
    cos = jnp.cos(ang)
    sin = jnp.sin(ang)
    cos_full = jnp.tile(cos, (1, V_HEAD_DIM // half))
    sin_full = jnp.tile(jnp.concatenate([-sin, sin], axis=-1), (1, V_HEAD_DIM // HEAD_DIM))
    return cos_full, sin_full


def kernel(x_prompt, x_sample, state_pool, cache_k, cache_v, page_table, norm_pool, w_in_pool, w_grp_pool, scale_pool, w_out_pool, norm_attn, w_in_attn, q_norm, k_norm, lambda_q1, lambda_k1, lambda_q2, lambda_k2, subln, w_out_attn):
    batch, seq, d = x_prompt.shape
    n_dec = x_sample.shape[0]
    depth = norm_pool.shape[0] + norm_attn.shape[0]
    past_len = page_table.shape[1] * PAGE_SIZE
    n_phys = cache_k.shape[1]

    cos_p, sin_p = _rope_tables(jnp.arange(seq))
    cos_s, sin_s = _rope_tables(jnp.full((n_dec,), past_len, jnp.int32))
    seg = jnp.arange(256) // HEAD_DIM
    ones_bd = (seg[:, None] == seg[None, :]).astype(BF16)
    n_layers = cache_k.shape[0]
    cache_kt = jnp.transpose(cache_k, (0, 1, 3, 4, 5, 2)).reshape(n_layers, n_phys, d, PAGE_SIZE)
    cache_vf = cache_v.reshape(n_layers, n_phys, PAGE_SIZE * N_HEADS, V_HEAD_DIM)
    state_t = jnp.transpose(state_pool, (0, 2, 1, 3))

    xp = x_prompt
    xs = x_sample.reshape(n_dec, d)
    ps_prompt, ps_sample, kp, vp, ksm, vsm = [], [], [], [], [], []
    for i in range(depth):
        j = i // 2
        if i % 2 == 0:
            g = norm_pool[j].reshape(1, d)
            w_in = w_in_pool[j].astype(BF16)
            w_grp = w_grp_pool[j].astype(BF16)
            scale = scale_pool[j].reshape(1, POOL_WIDTH)
            w_out = w_out_pool[j].astype(BF16)
            xp, st_p = _pool_prompt(xp, g, w_in, w_grp, scale, w_out)
            xs, st_s = _pool_sample(xs, state_t, j, g, w_in, w_grp, scale, w_out)
            ps_prompt.append(st_p)
            ps_sample.append(st_s)
        else:
            lam_init = 0.8 - 0.6 * math.exp(-0.3 * i)
            g = norm_attn[j].reshape(1, d)
            w_in = w_in_attn[j].astype(BF16)
            w_out = w_out_attn[j].astype(BF16)
            qn = jnp.tile(q_norm[j], d // HEAD_DIM).reshape(1, d)
            kn = jnp.tile(k_norm[j], d // HEAD_DIM).reshape(1, d)
            lams = [a[j].reshape(1, HEAD_DIM) for a in (lambda_q1, lambda_k1, lambda_q2, lambda_k2)]
            subg = subln[j].reshape(1, V_HEAD_DIM)

            x2 = xp.reshape(batch * seq, d)
            q, kb, kt, v, vt, z = _attn_proj(x2, g, w_in, ones_bd, qn, kn, cos_p, sin_p,
                                             tm=TM_PROJ, pos_tiles=seq // TM_PROJ)
            o = _flash_prompt(q, kb, vt, *lams, subg, batch=batch, seq=seq, lam_init=lam_init)
            xp = _attn_out(x2, o, z, w_out, tm=512).reshape(batch, seq, d)
            kp.append(kt)
            vp.append(v.reshape(batch, seq, N_HEADS, V_HEAD_DIM))

            qs, kb_s, kt_s, v_s, _, z_s = _attn_proj(xs, g, w_in, ones_bd, qn, kn, cos_s, sin_s,
                                                     tm=n_dec, pos_tiles=1)
            o_s = _decode_attn(page_table, qs, kb_s, v_s, cache_kt, cache_vf, j,
                               *lams, subg, lam_init=lam_init)
            xs = _attn_out(xs, o_s, z_s, w_out, tm=n_dec)
            ksm.append(kt_s)
            vsm.append(v_s.reshape(n_dec, 1, N_HEADS, V_HEAD_DIM))
    n_attn = len(kp)
    ps_sample = jnp.transpose(jnp.concatenate(ps_sample, axis=0), (0, 2, 1, 3))
    k_prompt = jnp.transpose(jnp.stack(kp).reshape(n_attn, batch, N_HEADS, 2, HEAD_DIM, seq),
                             (0, 1, 5, 2, 3, 4))
    k_sample = jnp.transpose(jnp.stack(ksm).reshape(n_attn, 1, N_HEADS, 2, HEAD_DIM, n_dec),
                             (0, 5, 1, 2, 3, 4))
    return (xp, xs.reshape(n_dec, 1, d), jnp.stack(ps_prompt), ps_sample,
            k_prompt, jnp.stack(vp), k_sample, jnp.stack(vsm))
```

```python
import functools
import math

import jax
import jax.numpy as jnp
from jax import lax
from jax.experimental import pallas as pl
from jax.experimental.pallas import tpu as pltpu

D_MODEL = 1024
POOL_WIDTH = 2048
POOL_WINDOWS = (2, 4, 8, 16)
POOL_GROUP_WIDTH = 512
POOL_STATE_LEN = 15
N_HEADS = 8
HEAD_DIM = 64
V_HEAD_DIM = 128
PAGE_SIZE = 128
ROPE_THETA = 10000.0
RMS_EPS = 1e-6
NEG_INF = -1e30

VMEM_LIMIT_BYTES = 56 * 1024 * 1024
HALO = 16
TM_PROJ = 256
TQ = 512
TK = TM_PROJ

F32 = jnp.float32
BF16 = jnp.bfloat16


def _params(*sem):
    return pltpu.CompilerParams(dimension_semantics=sem, vmem_limit_bytes=VMEM_LIMIT_BYTES)


def _rms_rows(x, g):
    ms = jnp.mean(x * x, axis=-1, keepdims=True)
    return x * lax.rsqrt(ms + RMS_EPS) * g


def _silu(z):
    return z * jax.nn.sigmoid(z)


def _dot(a, b):
    return jnp.dot(a, b, preferred_element_type=F32)


def _pool_prompt_kernel(x_ref, g_ref, win_ref, wgrp_ref, scale_ref, wout_ref,
                        y_ref, st_ref, ubuf, *, tm):
    t = pl.program_id(1)

    @pl.when(t == 0)
    def _():
        ubuf[0:HALO, :] = jnp.zeros((HALO, POOL_WIDTH), F32)

    x = x_ref[0]
    h = _rms_rows(x, g_ref[...]).astype(BF16)
    uz = _dot(h, win_ref[...])
    u = uz[:, :POOL_WIDTH]
    z = uz[:, POOL_WIDTH:]
    ubuf[HALO:HALO + tm, :] = u

    pos = t * tm + lax.broadcasted_iota(jnp.int32, (tm, 1), 0)
    ms = []
    for gi, w in enumerate(POOL_WINDOWS):
        cols = slice(gi * POOL_GROUP_WIDTH, (gi + 1) * POOL_GROUP_WIDTH)
        acc = ubuf[HALO:HALO + tm, cols]
        for i in range(1, w):
            acc = acc + ubuf[HALO - i:HALO - i + tm, cols]
        inv_cnt = 1.0 / jnp.minimum(pos + 1, w).astype(F32)
        r = acc * inv_cnt - u[:, cols]
        ms.append(_dot(r.astype(BF16), wgrp_ref[gi]))
    m = jnp.concatenate(ms, axis=-1) * scale_ref[...]
    gated = (m * _silu(z)).astype(BF16)
    y_ref[0] = x + _dot(gated, wout_ref[...])

    @pl.when(t == pl.num_programs(1) - 1)
    def _():
        st_ref[0] = ubuf[HALO + tm - POOL_STATE_LEN:HALO + tm, :]

    ubuf[0:HALO, :] = ubuf[tm:tm + HALO, :]


def _pool_prompt(x, g, w_in, w_grp, scale, w_out, tm=TM_PROJ):
    b, s, d = x.shape
    const = lambda *shape: pl.BlockSpec(shape, lambda i, j: (0,) * len(shape))
    return pl.pallas_call(
        functools.partial(_pool_prompt_kernel, tm=tm),
        out_shape=(jax.ShapeDtypeStruct((b, s, d), F32),
                   jax.ShapeDtypeStruct((b, POOL_STATE_LEN, POOL_WIDTH), F32)),
        grid=(b, s // tm),
        in_specs=[
            pl.BlockSpec((1, tm, d), lambda i, j: (i, j, 0)),
            const(1, d),
            const(d, 2 * POOL_WIDTH),
            const(len(POOL_WINDOWS), POOL_GROUP_WIDTH, POOL_GROUP_WIDTH),
            const(1, POOL_WIDTH),
            const(POOL_WIDTH, d),
        ],
        out_specs=(pl.BlockSpec((1, tm, d), lambda i, j: (i, j, 0)),
                   pl.BlockSpec((1, POOL_STATE_LEN, POOL_WIDTH), lambda i, j: (i, 0, 0))),
        scratch_shapes=[pltpu.VMEM((tm + HALO, POOL_WIDTH), F32)],
        compiler_params=_params("arbitrary", "arbitrary"),
        name="pool_prompt",
    )(x, g, w_in, w_grp, scale, w_out)


def _pool_sample_kernel(x_ref, st_ref, g_ref, win_ref, wgrp_ref, scale_ref, wout_ref,
                        y_ref, nst_ref):
    x = x_ref[...]
    h = _rms_rows(x, g_ref[...]).astype(BF16)
    uz = _dot(h, win_ref[...])
    u = uz[:, :POOL_WIDTH]
    z = uz[:, POOL_WIDTH:]
    nst_ref[0, 0:POOL_STATE_LEN - 1] = st_ref[0, 1:POOL_STATE_LEN]
    nst_ref[0, POOL_STATE_LEN - 1] = u
    ms = []
    for gi, w in enumerate(POOL_WINDOWS):
        cols = slice(gi * POOL_GROUP_WIDTH, (gi + 1) * POOL_GROUP_WIDTH)
        acc = u[:, cols]
        for i in range(1, w):
            acc = acc + st_ref[0, POOL_STATE_LEN - i, :, cols]
        r = acc * (1.0 / w) - u[:, cols]
        ms.append(_dot(r.astype(BF16), wgrp_ref[gi]))
    m = jnp.concatenate(ms, axis=-1) * scale_ref[...]
    gated = (m * _silu(z)).astype(BF16)
    y_ref[...] = x + _dot(gated, wout_ref[...])


def _pool_sample(x, state_t, layer, g, w_in, w_grp, scale, w_out, tb=32):
    n, d = x.shape
    const = lambda *shape: pl.BlockSpec(shape, lambda i: (0,) * len(shape))
    return pl.pallas_call(
        _pool_sample_kernel,
        out_shape=(jax.ShapeDtypeStruct((n, d), F32),
                   jax.ShapeDtypeStruct((1, POOL_STATE_LEN, n, POOL_WIDTH), F32)),
        grid=(n // tb,),
        in_specs=[
            pl.BlockSpec((tb, d), lambda i: (i, 0)),
            pl.BlockSpec((1, POOL_STATE_LEN, tb, POOL_WIDTH), lambda i: (layer, 0, i, 0)),
            const(1, d),
            const(d, 2 * POOL_WIDTH),
            const(len(POOL_WINDOWS), POOL_GROUP_WIDTH, POOL_GROUP_WIDTH),
            const(1, POOL_WIDTH),
            const(POOL_WIDTH, d),
        ],
        out_specs=(pl.BlockSpec((tb, d), lambda i: (i, 0)),
                   pl.BlockSpec((1, POOL_STATE_LEN, tb, POOL_WIDTH), lambda i: (0, 0, i, 0))),
        compiler_params=_params("arbitrary"),
        name="pool_sample",
    )(x, state_t, g, w_in, w_grp, scale, w_out)


def _seg_norm_rope(a, ones_ref, gain, cos, sin):
    sq = a * a
    hi = sq.astype(BF16)
    lo = (sq - hi.astype(F32)).astype(BF16)
    parts = []
    for c in range(a.shape[1] // 256):
        cs = slice(c * 256, (c + 1) * 256)
        parts.append(_dot(hi[:, cs], ones_ref[...]) + _dot(lo[:, cs], ones_ref[...]))
    ss = jnp.concatenate(parts, axis=-1)
    an = a * lax.rsqrt(ss * (1.0 / HEAD_DIM) + RMS_EPS) * gain
    lane = lax.broadcasted_iota(jnp.int32, (a.shape[0], V_HEAD_DIM), 1)
    first_half = (lane % HEAD_DIM) < (HEAD_DIM // 2)
    outs = []
    for hh in range(N_HEADS):
        xh = an[:, hh * V_HEAD_DIM:(hh + 1) * V_HEAD_DIM]
        swapped = jnp.where(first_half,
                            pltpu.roll(xh, V_HEAD_DIM - HEAD_DIM // 2, 1),
                            pltpu.roll(xh, HEAD_DIM // 2, 1))
        outs.append(xh * cos + swapped * sin)
    return jnp.concatenate(outs, axis=-1)


def _attn_proj_kernel(x_ref, g_ref, win_ref, ones_ref, qn_ref, kn_ref, cos_ref, sin_ref,
                      q_ref, kb_ref, kt_ref, v_ref, vt_ref, z_ref):
    x = x_ref[...]
    h = _rms_rows(x, g_ref[...]).astype(BF16)
    proj = _dot(h, win_ref[...])
    cos = cos_ref[...]
    sin = sin_ref[...]
    q = _seg_norm_rope(proj[:, 0:D_MODEL], ones_ref, qn_ref[...], cos, sin)
    k = _seg_norm_rope(proj[:, D_MODEL:2 * D_MODEL], ones_ref, kn_ref[...], cos, sin)
    v = proj[:, 2 * D_MODEL:3 * D_MODEL]
    q_ref[...] = (q * (HEAD_DIM ** -0.5 * math.log2(math.e))).astype(BF16)
    kb_ref[...] = k.astype(BF16)
    kt_ref[0] = k.T
    v_ref[...] = v.reshape(v.shape[0], N_HEADS, V_HEAD_DIM)
    vt_ref[:, 0] = v.T.astype(BF16).reshape(N_HEADS, V_HEAD_DIM, v.shape[0])
    z_ref[...] = proj[:, 3 * D_MODEL:]


def _attn_proj(x2d, g, w_in, ones, qn_full, kn_full, cos, sin, *, tm, pos_tiles):
    r, d = x2d.shape
    nt = r // tm
    const = lambda *shape: pl.BlockSpec(shape, lambda i: (0,) * len(shape))
    row = lambda width: pl.BlockSpec((tm, width), lambda i: (i, 0))
    tab = pl.BlockSpec((tm, V_HEAD_DIM), lambda i: (i % pos_tiles, 0))
    return pl.pallas_call(
        _attn_proj_kernel,
        out_shape=(jax.ShapeDtypeStruct((r, d), BF16),
                   jax.ShapeDtypeStruct((r, d), BF16),
                   jax.ShapeDtypeStruct((nt // pos_tiles, d, pos_tiles * tm), F32),
                   jax.ShapeDtypeStruct((r, N_HEADS, V_HEAD_DIM), F32),
                   jax.ShapeDtypeStruct((N_HEADS, nt, V_HEAD_DIM, tm), BF16),
                   jax.ShapeDtypeStruct((r, d), F32)),
        grid=(nt,),
        in_specs=[row(d), const(1, d), const(d, 4 * d), const(256, 256),
                  const(1, d), const(1, d), tab, tab],
        out_specs=(row(d), row(d),
                   pl.BlockSpec((1, d, tm), lambda i: (i // pos_tiles, 0, i % pos_tiles)),
                   pl.BlockSpec((tm, N_HEADS, V_HEAD_DIM), lambda i: (i, 0, 0)),
                   pl.BlockSpec((N_HEADS, 1, V_HEAD_DIM, tm), lambda i: (0, i, 0, 0)),
                   row(d)),
        compiler_params=_params("arbitrary"),
        name="attn_proj",
    )(x2d, g, w_in, ones, qn_full, kn_full, cos, sin)


def _lambda(lq1, lk1, lq2, lk2, lam_init):
    a = jnp.sum(lq1 * lk1, axis=-1, keepdims=True)
    b = jnp.sum(lq2 * lk2, axis=-1, keepdims=True)
    return jnp.exp(a) - jnp.exp(b) + lam_init


def _attn_core_kernel(pt_ref, lq1_ref, lk1_ref, lq2_ref, lk2_ref, subg_ref, q_ref, k_ref, vt_ref,
                      qd_ref, knd_ref, vnd_ref, expand_ref, *rest, n_pages, lam_init):
    k_pages = rest[:n_pages]
    v_pages = rest[n_pages:2 * n_pages]
    o_ref, od_ref, qc_s, s_a, s_b, acc_s, a_s, aown_s = rest[2 * n_pages:]
    del pt_ref
    lam = _lambda(lq1_ref[...], lk1_ref[...], lq2_ref[...], lk2_ref[...], lam_init)
    step = (pl.program_id(0) * pl.num_programs(1) + pl.program_id(1)) * pl.num_programs(2) \
        + pl.program_id(2)

    @pl.when(step % 2 == 0)
    def _():
        _decode_keys(lam, qd_ref, knd_ref, k_pages, a_s, aown_s)

    @pl.when(step % 2 == 1)
    def _():
        _decode_values(subg_ref, vnd_ref, expand_ref, v_pages, a_s, aown_s, od_ref,
                       lam_init=lam_init)

    _flash_tile(lam, subg_ref, q_ref, k_ref, vt_ref, o_ref, qc_s, s_a, s_b, acc_s,
                lam_init=lam_init)


def _flash_tile(lam, subg_ref, q_ref, k_ref, vt_ref, o_ref, qc_s, s_a, s_b, acc_s, *, lam_init):
    qi = pl.program_id(2)
    q = q_ref[...]
    lane = lax.broadcasted_iota(jnp.int32, q.shape, 1)
    zero = jnp.zeros_like(q)
    qc_s[0] = jnp.where(lane < HEAD_DIM, q, zero)
    qc_s[1] = jnp.where(lane >= HEAD_DIM, q, zero)
    acc_s[...] = jnp.zeros(acc_s.shape, F32)

    def scores(ki, s_ref):
        k = k_ref[pl.ds(pl.multiple_of(ki * TQ, TQ), TQ), :]
        for c in range(2):
            s_ref[c] = lax.dot_general(k, qc_s[c], (((1,), (1,)), ((), ())),
                                       preferred_element_type=F32)

    def softmax_pv(ki, s_ref, carry, masked):
        if masked:
            keep = (lax.broadcasted_iota(jnp.int32, (TQ, TQ), 0)
                    <= lax.broadcasted_iota(jnp.int32, (TQ, TQ), 1))
        out = []
        for c in range(2):
            m_old, l_old = carry[c]
            s = s_ref[c]
            if masked:
                s = jnp.where(keep, s, NEG_INF)
            m_new = jnp.maximum(m_old, jnp.max(s, axis=0, keepdims=True))
            alpha = jnp.exp2(m_old - m_new)
            p = jnp.exp2(s - m_new)
            l_new = alpha * l_old + jnp.sum(p, axis=0, keepdims=True)
            pb = p.astype(BF16)
            sub = TQ // TK
            pv = _dot(vt_ref[0, sub * ki], pb[0:TK])
            for j in range(1, sub):
                pv = pv + _dot(vt_ref[0, sub * ki + j], pb[j * TK:(j + 1) * TK])
            acc_s[c] = alpha * acc_s[c] + pv
            out.append((m_new, l_new))
        return tuple(out)

    def body(kk, carry):
        k0 = 2 * kk
        scores(k0 + 1, s_b)
        carry = softmax_pv(k0, s_a, carry, False)
        scores(k0 + 2, s_a)
        return softmax_pv(k0 + 1, s_b, carry, False)

    def last_even(carry):
        return softmax_pv(qi, s_a, carry, True)

    def last_odd(carry):
        scores(qi, s_b)
        return softmax_pv(qi, s_b, softmax_pv(qi - 1, s_a, carry, False), True)

    neg = jnp.full((1, TQ), NEG_INF, F32)
    zero_l = jnp.zeros((1, TQ), F32)
    scores(0, s_a)
    carry = lax.fori_loop(0, qi // 2, body, ((neg, zero_l), (neg, zero_l)))
    carry = lax.cond(qi % 2 == 0, last_even, last_odd, carry)

    ot = acc_s[0] / carry[0][1] - lam * (acc_s[1] / carry[1][1])
    o = ot.T
    o_ref[...] = _rms_rows(o, subg_ref[...]) * (1.0 - lam_init)


def _attn_core(page_table, q, kb, vt, qd, knd, vnd, cache_kt, cache_vf, layer,
               lq1, lk1, lq2, lk2, subg, *, batch, seq, lam_init):
    n, n_pages = page_table.shape
    nq = seq // TQ
    nk = seq // TK
    assert batch * N_HEADS * nq == 2 * n

    def lin(b, h, i):
        return (b * N_HEADS + h) * nq + i

    vec = lambda w: pl.BlockSpec((1, w), lambda b, h, i, pt: (0, 0))
    row = pl.BlockSpec((1, 1, D_MODEL), lambda b, h, i, pt: (lin(b, h, i) // 2, 0, 0))
    per_head = pl.BlockSpec((1, N_HEADS, V_HEAD_DIM),
                            lambda b, h, i, pt: (lin(b, h, i) // 2, 0, 0))

    def key_page(slot):
        return pl.BlockSpec((1, 1, D_MODEL, PAGE_SIZE),
                            lambda b, h, i, pt: (layer, pt[lin(b, h, i) // 2, slot], 0, 0))

    def value_page(slot):
        return pl.BlockSpec(
            (1, 1, D_MODEL, PAGE_SIZE),
            lambda b, h, i, pt: (layer, pt[jnp.maximum(lin(b, h, i) - 1, 0) // 2, slot], 0, 0))

    key = jnp.arange(PAGE_SIZE)
    expand = (key[:, None] == (jnp.arange(D_MODEL) // N_HEADS)[None, :]).astype(BF16)
    in_specs = ([vec(HEAD_DIM)] * 4 + [vec(V_HEAD_DIM)]
                + [pl.BlockSpec((TQ, V_HEAD_DIM), lambda b, h, i, pt: (b * nq + i, h)),
                   pl.BlockSpec((seq, V_HEAD_DIM), lambda b, h, i, pt: (b, h)),
                   pl.BlockSpec((1, nk, V_HEAD_DIM, TK), lambda b, h, i, pt: (h, b, 0, 0)),
                   row, row, per_head,
                   pl.BlockSpec((PAGE_SIZE, D_MODEL), lambda b, h, i, pt: (0, 0))]
                + [key_page(s) for s in range(n_pages)]
                + [value_page(s) for s in range(n_pages)])
    o, od = pl.pallas_call(
        functools.partial(_attn_core_kernel, n_pages=n_pages, lam_init=lam_init),
        out_shape=(jax.ShapeDtypeStruct((batch * seq, N_HEADS * V_HEAD_DIM), F32),
                   jax.ShapeDtypeStruct((n, N_HEADS, V_HEAD_DIM), F32)),
        grid_spec=pltpu.PrefetchScalarGridSpec(
            num_scalar_prefetch=1,
            grid=(batch, N_HEADS, nq),
            in_specs=in_specs,
            out_specs=(pl.BlockSpec((TQ, V_HEAD_DIM), lambda b, h, i, pt: (b * nq + i, h)),
                       per_head),
            scratch_shapes=[pltpu.VMEM((2, TQ, V_HEAD_DIM), BF16),
                            pltpu.VMEM((2, TQ, TQ), F32),
                            pltpu.VMEM((2, TQ, TQ), F32),
                            pltpu.VMEM((2, V_HEAD_DIM, TQ), F32),
                            pltpu.VMEM((N_HEADS, n_pages * PAGE_SIZE), F32),
                            pltpu.VMEM((N_HEADS, V_HEAD_DIM), F32)],
        ),
        compiler_params=_params("arbitrary", "arbitrary", "arbitrary"),
        name="attn_core",
    )(page_table, lq1, lk1, lq2, lk2, subg, q, kb, vt,
      qd.reshape(n, 1, D_MODEL), knd.reshape(n, 1, D_MODEL), vnd, expand,
      *([cache_kt] * n_pages), *([cache_vf] * n_pages))
    return o, od.reshape(n, D_MODEL)


def _decode_keys(lam, q_ref, kn_ref, k_refs, a_s, aown_s):
    nseg = 2 * N_HEADS
    q = q_ref[0].astype(F32)
    seg = lax.broadcasted_iota(jnp.int32, (nseg, D_MODEL), 1) // HEAD_DIM
    rowi = lax.broadcasted_iota(jnp.int32, (nseg, D_MODEL), 0)
    row_seg = 2 * (rowi % N_HEADS) + rowi // N_HEADS
    qbd = jnp.where(seg == row_seg, jnp.broadcast_to(q, (nseg, D_MODEL)), 0.0)
    qbd16 = qbd.astype(BF16)
    s = jnp.concatenate([_dot(qbd16, k_ref[0, 0].astype(BF16)) for k_ref in k_refs],
                        axis=1)
    kn = kn_ref[0].astype(F32)
    s_own = jnp.sum(qbd * kn, axis=-1, keepdims=True)
    m = jnp.maximum(jnp.max(s, axis=-1, keepdims=True), s_own)
    p = jnp.exp2(s - m)
    p_own = jnp.exp2(s_own - m)
    inv_l = 1.0 / (jnp.sum(p, axis=-1, keepdims=True) + p_own)
    p = p * inv_l
    p_own = p_own * inv_l
    a_s[...] = p[:N_HEADS] - lam * p[N_HEADS:]
    aown_s[...] = jnp.broadcast_to(p_own[:N_HEADS] - lam * p_own[N_HEADS:], aown_s.shape)


def _decode_values(subg_ref, vn_ref, expand_ref, v_refs, a_s, aown_s, o_ref, *, lam_init):
    n_pages = len(v_refs)
    a = a_s[...]
    a_rows = jnp.concatenate([a[:, i * PAGE_SIZE:(i + 1) * PAGE_SIZE] for i in range(n_pages)],
                             axis=0).astype(BF16)
    w = _dot(a_rows, expand_ref[...])
    same_head = (lax.broadcasted_iota(jnp.int32, w.shape, 0) % N_HEADS
                 == lax.broadcasted_iota(jnp.int32, w.shape, 1) % N_HEADS)
    w = jnp.where(same_head, w, 0.0).astype(BF16)
    vn = vn_ref[0].astype(BF16).astype(F32)
    o = aown_s[...].astype(BF16).astype(F32) * vn
    for i, v_ref in enumerate(v_refs):
        o = o + _dot(w[i * N_HEADS:(i + 1) * N_HEADS], v_ref[0, 0].astype(BF16))
    o_ref[0] = _rms_rows(o, subg_ref[...]) * (1.0 - lam_init)


def _attn_out_kernel(x_ref, o_ref, z_ref, w_ref, y_ref):
    gated = (o_ref[...] * _silu(z_ref[...])).astype(BF16)
    y_ref[...] = x_ref[...] + _dot(gated, w_ref[...])


def _attn_out(x2d, o, z, w_out, *, tm):
    r, d = x2d.shape
    row = pl.BlockSpec((tm, d), lambda i: (i, 0))
    return pl.pallas_call(
        _attn_out_kernel,
        out_shape=jax.ShapeDtypeStruct((r, d), F32),
        grid=(r // tm,),
        in_specs=[row, row, row, pl.BlockSpec((d, d), lambda i: (0, 0))],
        out_specs=row,
        compiler_params=_params("arbitrary"),
        name="attn_out",
    )(x2d, o, z, w_out)


def _rope_tables(pos):
    half = HEAD_DIM // 2
    inv = jnp.power(ROPE_THETA, -jnp.arange(0, HEAD_DIM, 2, dtype=F32) / HEAD_DIM)
    ang = pos.astype(F32)[:, None] * inv[None, :]
    cos = jnp.cos(ang)
    sin = jnp.sin(ang)
    cos_full = jnp.tile(cos, (1, V_HEAD_DIM // half))
    sin_full = jnp.tile(jnp.concatenate([-sin, sin], axis=-1), (1, V_HEAD_DIM // HEAD_DIM))
    return cos_full, sin_full


def kernel(x_prompt, x_sample, state_pool, cache_k, cache_v, page_table, norm_pool, w_in_pool, w_grp_pool, scale_pool, w_out_pool, norm_attn, w_in_attn, q_norm, k_norm, lambda_q1, lambda_k1, lambda_q2, lambda_k2, subln, w_out_attn):
    batch, seq, d = x_prompt.shape
    n_dec = x_sample.shape[0]
    depth = norm_pool.shape[0] + norm_attn.shape[0]
    past_len = page_table.shape[1] * PAGE_SIZE
    n_phys = cache_k.shape[1]

    cos_p, sin_p = _rope_tables(jnp.arange(seq))
    cos_s, sin_s = _rope_tables(jnp.full((n_dec,), past_len, jnp.int32))
    seg = jnp.arange(256) // HEAD_DIM
    ones_bd = (seg[:, None] == seg[None, :]).astype(BF16)
    n_layers = cache_k.shape[0]
    cache_kt = jnp.transpose(cache_k, (0, 1, 3, 4, 5, 2)).reshape(n_layers, n_phys, d, PAGE_SIZE)
    cache_vf = cache_v.reshape(n_layers, n_phys, PAGE_SIZE * N_HEADS, V_HEAD_DIM)
    state_t = jnp.transpose(state_pool, (0, 2, 1, 3))

    xp = x_prompt
    xs = x_sample.reshape(n_dec, d)
    ps_prompt, ps_sample, kp, vp, ksm, vsm = [], [], [], [], [], []
    for i in range(depth):
        j = i // 2
        if i % 2 == 0:
            g = norm_pool[j].reshape(1, d)
            w_in = w_in_pool[j].astype(BF16)
            w_grp = w_grp_pool[j].astype(BF16)
            scale = scale_pool[j].reshape(1, POOL_WIDTH)
            w_out = w_out_pool[j].astype(BF16)
            xp, st_p = _pool_prompt(xp, g, w_in, w_grp, scale, w_out)
            xs, st_s = _pool_sample(xs, state_t, j, g, w_in, w_grp, scale, w_out)
            ps_prompt.append(st_p)
            ps_sample.append(st_s)
        else:
            lam_init = 0.8 - 0.6 * math.exp(-0.3 * i)
            g = norm_attn[j].reshape(1, d)
            w_in = w_in_attn[j].astype(BF16)
            w_out = w_out_attn[j].astype(BF16)
            qn = jnp.tile(q_norm[j], d // HEAD_DIM).reshape(1, d)
            kn = jnp.tile(k_norm[j], d // HEAD_DIM).reshape(1, d)
            lams = [a[j].reshape(1, HEAD_DIM) for a in (lambda_q1, lambda_k1, lambda_q2, lambda_k2)]
            subg = subln[j].reshape(1, V_HEAD_DIM)

            x2 = xp.reshape(batch * seq, d)
            q, kb, kt, v, vt, z = _attn_proj(x2, g, w_in, ones_bd, qn, kn, cos_p, sin_p,
                                             tm=TM_PROJ, pos_tiles=seq // TM_PROJ)
            qs, kb_s, kt_s, v_s, _, z_s = _attn_proj(xs, g, w_in, ones_bd, qn, kn, cos_s, sin_s,
                                                     tm=n_dec, pos_tiles=1)
            o, o_s = _attn_core(page_table, q, kb, vt, qs, kb_s, v_s, cache_kt, cache_vf, j,
                                *lams, subg, batch=batch, seq=seq, lam_init=lam_init)
            xp = _attn_out(x2, o, z, w_out, tm=512).reshape(batch, seq, d)
            xs = _attn_out(xs, o_s, z_s, w_out, tm=n_dec)
            kp.append(kt)
            vp.append(v.reshape(batch, seq, N_HEADS, V_HEAD_DIM))
            ksm.append(kt_s)
            vsm.append(v_s.reshape(n_dec, 1, N_HEADS, V_HEAD_DIM))
    n_attn = len(kp)
    ps_sample = jnp.transpose(jnp.concatenate(ps_sample, axis=0), (0, 2, 1, 3))
    k_prompt = jnp.transpose(jnp.stack(kp).reshape(n_attn, batch, N_HEADS, 2, HEAD_DIM, seq),
                             (0, 1, 5, 2, 3, 4))
    k_sample = jnp.transpose(jnp.stack(ksm).reshape(n_attn, 1, N_HEADS, 2, HEAD_DIM, n_dec),
                             (0, 5, 1, 2, 3, 4))
    return (xp, xs.reshape(n_dec, 1, d), jnp.stack(ps_prompt), ps_sample,
            k_prompt, jnp.stack(vp), k_sample, jnp.stack(vsm))
```

```python
import functools
import math

import jax
import jax.numpy as jnp
from jax import lax
from jax.experimental import pallas as pl
from jax.experimental.pallas import tpu as pltpu

D_MODEL = 1024
POOL_WIDTH = 2048
POOL_WINDOWS = (2, 4, 8, 16)
POOL_GROUP_WIDTH = 512
POOL_STATE_LEN = 15
N_HEADS = 8
HEAD_DIM = 64
V_HEAD_DIM = 128
PAGE_SIZE = 128
ROPE_THETA = 10000.0
RMS_EPS = 1e-6
NEG_INF = -1e30

VMEM_LIMIT_BYTES = 56 * 1024 * 1024
HALO = 16
TM_PROJ = 256
TQ = 512
TK = TM_PROJ

F32 = jnp.float32
BF16 = jnp.bfloat16


def _params(*sem):
    return pltpu.CompilerParams(dimension_semantics=sem, vmem_limit_bytes=VMEM_LIMIT_BYTES)


def _rms_rows(x, g):
    ms = jnp.mean(x * x, axis=-1, keepdims=True)
    return x * lax.rsqrt(ms + RMS_EPS) * g


def _silu(z):
    return z * jax.nn.sigmoid(z)


def _dot(a, b):
    return jnp.dot(a, b, preferred_element_type=F32)


def _pool_prompt_kernel(x_ref, g_ref, win_ref, wgrp_ref, scale_ref, wout_ref,
                        y_ref, st_ref, ubuf, *, tm):
    t = pl.program_id(1)

    @pl.when(t == 0)
    def _():
        ubuf[0:HALO, :] = jnp.zeros((HALO, POOL_WIDTH), F32)

    x = x_ref[0]
    h = _rms_rows(x, g_ref[...]).astype(BF16)
    uz = _dot(h, win_ref[...])
    u = uz[:, :POOL_WIDTH]
    z = uz[:, POOL_WIDTH:]
    ubuf[HALO:HALO + tm, :] = u

    pos = t * tm + lax.broadcasted_iota(jnp.int32, (tm, 1), 0)
    ms = []
    for gi, w in enumerate(POOL_WINDOWS):
        cols = slice(gi * POOL_GROUP_WIDTH, (gi + 1) * POOL_GROUP_WIDTH)
        acc = ubuf[HALO:HALO + tm, cols]
        for i in range(1, w):
            acc = acc + ubuf[HALO - i:HALO - i + tm, cols]
        inv_cnt = 1.0 / jnp.minimum(pos + 1, w).astype(F32)
        r = acc * inv_cnt - u[:, cols]
        ms.append(_dot(r.astype(BF16), wgrp_ref[gi]))
    m = jnp.concatenate(ms, axis=-1) * scale_ref[...]
    gated = (m * _silu(z)).astype(BF16)
    y_ref[0] = x + _dot(gated, wout_ref[...])

    @pl.when(t == pl.num_programs(1) - 1)
    def _():
        st_ref[0] = ubuf[HALO + tm - POOL_STATE_LEN:HALO + tm, :]

    ubuf[0:HALO, :] = ubuf[tm:tm + HALO, :]


def _pool_prompt(x, g, w_in, w_grp, scale, w_out, tm=TM_PROJ):
    b, s, d = x.shape
    const = lambda *shape: pl.BlockSpec(shape, lambda i, j: (0,) * len(shape))
    return pl.pallas_call(
        functools.partial(_pool_prompt_kernel, tm=tm),
        out_shape=(jax.ShapeDtypeStruct((b, s, d), F32),
                   jax.ShapeDtypeStruct((b, POOL_STATE_LEN, POOL_WIDTH), F32)),
        grid=(b, s // tm),
        in_specs=[
            pl.BlockSpec((1, tm, d), lambda i, j: (i, j, 0)),
            const(1, d),
            const(d, 2 * POOL_WIDTH),
            const(len(POOL_WINDOWS), POOL_GROUP_WIDTH, POOL_GROUP_WIDTH),
            const(1, POOL_WIDTH),
            const(POOL_WIDTH, d),
        ],
        out_specs=(pl.BlockSpec((1, tm, d), lambda i, j: (i, j, 0)),
                   pl.BlockSpec((1, POOL_STATE_LEN, POOL_WIDTH), lambda i, j: (i, 0, 0))),
        scratch_shapes=[pltpu.VMEM((tm + HALO, POOL_WIDTH), F32)],
        compiler_params=_params("arbitrary", "arbitrary"),
        name="pool_prompt",
    )(x, g, w_in, w_grp, scale, w_out)


def _pool_sample_kernel(x_ref, st_ref, g_ref, win_ref, wgrp_ref, scale_ref, wout_ref,
                        y_ref, nst_ref):
    x = x_ref[...]
    h = _rms_rows(x, g_ref[...]).astype(BF16)
    uz = _dot(h, win_ref[...])
    u = uz[:, :POOL_WIDTH]
    z = uz[:, POOL_WIDTH:]
    nst_ref[0, 0:POOL_STATE_LEN - 1] = st_ref[0, 1:POOL_STATE_LEN]
    nst_ref[0, POOL_STATE_LEN - 1] = u
    ms = []
    for gi, w in enumerate(POOL_WINDOWS):
        cols = slice(gi * POOL_GROUP_WIDTH, (gi + 1) * POOL_GROUP_WIDTH)
        acc = u[:, cols]
        for i in range(1, w):
            acc = acc + st_ref[0, POOL_STATE_LEN - i, :, cols]
        r = acc * (1.0 / w) - u[:, cols]
        ms.append(_dot(r.astype(BF16), wgrp_ref[gi]))
    m = jnp.concatenate(ms, axis=-1) * scale_ref[...]
    gated = (m * _silu(z)).astype(BF16)
    y_ref[...] = x + _dot(gated, wout_ref[...])


def _pool_sample(x, state_t, layer, g, w_in, w_grp, scale, w_out, tb=32):
    n, d = x.shape
    const = lambda *shape: pl.BlockSpec(shape, lambda i: (0,) * len(shape))
    return pl.pallas_call(
        _pool_sample_kernel,
        out_shape=(jax.ShapeDtypeStruct((n, d), F32),
                   jax.ShapeDtypeStruct((1, POOL_STATE_LEN, n, POOL_WIDTH), F32)),
        grid=(n // tb,),
        in_specs=[
            pl.BlockSpec((tb, d), lambda i: (i, 0)),
            pl.BlockSpec((1, POOL_STATE_LEN, tb, POOL_WIDTH), lambda i: (layer, 0, i, 0)),
            const(1, d),
            const(d, 2 * POOL_WIDTH),
            const(len(POOL_WINDOWS), POOL_GROUP_WIDTH, POOL_GROUP_WIDTH),
            const(1, POOL_WIDTH),
            const(POOL_WIDTH, d),
        ],
        out_specs=(pl.BlockSpec((tb, d), lambda i: (i, 0)),
                   pl.BlockSpec((1, POOL_STATE_LEN, tb, POOL_WIDTH), lambda i: (0, 0, i, 0))),
        compiler_params=_params("arbitrary"),
        name="pool_sample",
    )(x, state_t, g, w_in, w_grp, scale, w_out)


def _seg_norm_rope(a, ones_ref, gain, cos, sin):
    sq = a * a
    hi = sq.astype(BF16)
    lo = (sq - hi.astype(F32)).astype(BF16)
    parts = []
    for c in range(a.shape[1] // 256):
        cs = slice(c * 256, (c + 1) * 256)
        parts.append(_dot(hi[:, cs], ones_ref[...]) + _dot(lo[:, cs], ones_ref[...]))
    ss = jnp.concatenate(parts, axis=-1)
    an = a * lax.rsqrt(ss * (1.0 / HEAD_DIM) + RMS_EPS) * gain
    lane = lax.broadcasted_iota(jnp.int32, (a.shape[0], V_HEAD_DIM), 1)
    first_half = (lane % HEAD_DIM) < (HEAD_DIM // 2)
    outs = []
    for hh in range(N_HEADS):
        xh = an[:, hh * V_HEAD_DIM:(hh + 1) * V_HEAD_DIM]
        swapped = jnp.where(first_half,
                            pltpu.roll(xh, V_HEAD_DIM - HEAD_DIM // 2, 1),
                            pltpu.roll(xh, HEAD_DIM // 2, 1))
        outs.append(xh * cos + swapped * sin)
    return jnp.concatenate(outs, axis=-1)


def _attn_proj_kernel(x_ref, g_ref, win_ref, ones_ref, qn_ref, kn_ref, cos_ref, sin_ref, *rest,
                      n_prev):
    prev_kt = rest[:n_prev]
    prev_v = rest[n_prev:2 * n_prev]
    q_ref, kb_ref, kt_ref, v_ref, vt_ref, z_ref = rest[2 * n_prev:]
    for j in range(n_prev):
        kt_ref[j] = prev_kt[j][0]
        v_ref[j] = prev_v[j][0]
    x = x_ref[...]
    h = _rms_rows(x, g_ref[...]).astype(BF16)
    proj = _dot(h, win_ref[...])
    cos = cos_ref[...]
    sin = sin_ref[...]
    q = _seg_norm_rope(proj[:, 0:D_MODEL], ones_ref, qn_ref[...], cos, sin)
    k = _seg_norm_rope(proj[:, D_MODEL:2 * D_MODEL], ones_ref, kn_ref[...], cos, sin)
    v = proj[:, 2 * D_MODEL:3 * D_MODEL]
    q_ref[...] = (q * (HEAD_DIM ** -0.5 * math.log2(math.e))).astype(BF16)
    kb_ref[...] = k.astype(BF16)
    kt_ref[n_prev, 0] = k.T
    v_ref[n_prev] = v.reshape(v.shape[0], N_HEADS, V_HEAD_DIM)
    vt_ref[:, 0] = v.T.astype(BF16).reshape(N_HEADS, V_HEAD_DIM, v.shape[0])
    z_ref[...] = proj[:, 3 * D_MODEL:]


def _attn_proj(x2d, g, w_in, ones, qn_full, kn_full, cos, sin, prev_kt=(), prev_v=(), *,
               tm, pos_tiles):
    r, d = x2d.shape
    nt = r // tm
    n_prev = len(prev_kt)
    const = lambda *shape: pl.BlockSpec(shape, lambda i: (0,) * len(shape))
    row = lambda width: pl.BlockSpec((tm, width), lambda i: (i, 0))
    tab = pl.BlockSpec((tm, V_HEAD_DIM), lambda i: (i % pos_tiles, 0))
    kt_spec = lambda layers: pl.BlockSpec((layers, 1, d, tm),
                                          lambda i: (0, i // pos_tiles, 0, i % pos_tiles))
    v_spec = lambda layers: pl.BlockSpec((layers, tm, N_HEADS, V_HEAD_DIM), lambda i: (0, i, 0, 0))
    return pl.pallas_call(
        functools.partial(_attn_proj_kernel, n_prev=n_prev),
        out_shape=(jax.ShapeDtypeStruct((r, d), BF16),
                   jax.ShapeDtypeStruct((r, d), BF16),
                   jax.ShapeDtypeStruct((n_prev + 1, nt // pos_tiles, d, pos_tiles * tm), F32),
                   jax.ShapeDtypeStruct((n_prev + 1, r, N_HEADS, V_HEAD_DIM), F32),
                   jax.ShapeDtypeStruct((N_HEADS, nt, V_HEAD_DIM, tm), BF16),
                   jax.ShapeDtypeStruct((r, d), F32)),
        grid=(nt,),
        in_specs=([row(d), const(1, d), const(d, 4 * d), const(256, 256),
                   const(1, d), const(1, d), tab, tab]
                  + [kt_spec(1)] * n_prev + [v_spec(1)] * n_prev),
        out_specs=(row(d), row(d), kt_spec(n_prev + 1), v_spec(n_prev + 1),
                   pl.BlockSpec((N_HEADS, 1, V_HEAD_DIM, tm), lambda i: (0, i, 0, 0)),
                   row(d)),
        compiler_params=_params("arbitrary"),
        name="attn_proj",
    )(x2d, g, w_in, ones, qn_full, kn_full, cos, sin, *prev_kt, *prev_v)


def _lambda(lq1, lk1, lq2, lk2, lam_init):
    a = jnp.sum(lq1 * lk1, axis=-1, keepdims=True)
    b = jnp.sum(lq2 * lk2, axis=-1, keepdims=True)
    return jnp.exp(a) - jnp.exp(b) + lam_init


def _attn_core_kernel(pt_ref, lq1_ref, lk1_ref, lq2_ref, lk2_ref, subg_ref, q_ref, k_ref, vt_ref,
                      qd_ref, knd_ref, vnd_ref, expand_ref, kcache_ref, vcache_ref,
                      o_ref, od_ref, qc_s, s_a, s_b, mx_a, mx_b, acc_s, a_s, aown_s, pages, sem,
                      *, layer, n_pages, lam_init):
    lam = _lambda(lq1_ref[...], lk1_ref[...], lq2_ref[...], lk2_ref[...], lam_init)
    n_steps = pl.num_programs(0) * pl.num_programs(1) * pl.num_programs(2)
    step = (pl.program_id(0) * pl.num_programs(1) + pl.program_id(1)) * pl.num_programs(2) \
        + pl.program_id(2)
    row = step // 2
    key_pages = [pages.at[0, i] for i in range(n_pages)]
    value_pages = [pages.at[1, i] for i in range(n_pages)]

    def key_copy(r, i):
        return pltpu.make_async_copy(kcache_ref.at[layer, pt_ref[r, i]], key_pages[i], sem.at[0])

    def value_copy(r, i):
        return pltpu.make_async_copy(vcache_ref.at[layer, pt_ref[r, i]], value_pages[i], sem.at[1])

    @pl.when(step == 0)
    def _():
        for i in range(n_pages):
            key_copy(0, i).start()

    @pl.when(step % 2 == 0)
    def _():
        for i in range(n_pages):
            value_copy(row, i).start()
        for i in range(n_pages):
            key_copy(row, i).wait()
        _decode_keys(lam, qd_ref, knd_ref, key_pages, a_s, aown_s)

    @pl.when(step % 2 == 1)
    def _():
        @pl.when(step + 1 < n_steps)
        def _():
            for i in range(n_pages):
                key_copy(row + 1, i).start()
        for i in range(n_pages):
            value_copy(row, i).wait()
        _decode_values(subg_ref, vnd_ref, expand_ref, value_pages, a_s, aown_s, od_ref,
                       lam_init=lam_init)

    _flash_tile(lam, subg_ref, q_ref, k_ref, vt_ref, o_ref, qc_s, s_a, s_b, mx_a, mx_b, acc_s,
                lam_init=lam_init)


def _flash_tile(lam, subg_ref, q_ref, k_ref, vt_ref, o_ref, qc_s, s_a, s_b, mx_a, mx_b, acc_s,
                *, lam_init):
    qi = pl.program_id(2)
    q = q_ref[...]
    lane = lax.broadcasted_iota(jnp.int32, q.shape, 1)
    zero = jnp.zeros_like(q)
    qc_s[0] = jnp.where(lane < HEAD_DIM, q, zero)
    qc_s[1] = jnp.where(lane >= HEAD_DIM, q, zero)
    acc_s[...] = jnp.zeros(acc_s.shape, F32)

    def scores(ki, sm_ref):
        s_ref, mx_ref = sm_ref
        k = k_ref[pl.ds(pl.multiple_of(ki * TQ, TQ), TQ), :]
        for c in range(2):
            s = lax.dot_general(k, qc_s[c], (((1,), (1,)), ((), ())),
                                preferred_element_type=F32)
            s_ref[c] = s
            mx_ref[c] = jnp.max(s, axis=0, keepdims=True)

    def softmax_pv(ki, sm_ref, carry, masked):
        s_ref, mx_ref = sm_ref
        if masked:
            keep = (lax.broadcasted_iota(jnp.int32, (TQ, TQ), 0)
                    <= lax.broadcasted_iota(jnp.int32, (TQ, TQ), 1))
        out = []
        for c in range(2):
            m_old, l_old = carry[c]
            s = s_ref[c]
            if masked:
                s = jnp.where(keep, s, NEG_INF)
                m_tile = jnp.max(s, axis=0, keepdims=True)
            else:
                m_tile = mx_ref[c]
            m_new = jnp.maximum(m_old, m_tile)
            alpha = jnp.exp2(m_old - m_new)
            p = jnp.exp2(s - m_new)
            l_new = alpha * l_old + jnp.sum(p, axis=0, keepdims=True)
            pb = p.astype(BF16)
            sub = TQ // TK
            pv = _dot(vt_ref[0, sub * ki], pb[0:TK])
            for j in range(1, sub):
                pv = pv + _dot(vt_ref[0, sub * ki + j], pb[j * TK:(j + 1) * TK])
            acc_s[c] = alpha * acc_s[c] + pv
            out.append((m_new, l_new))
        return tuple(out)

    buf_a = (s_a, mx_a)
    buf_b = (s_b, mx_b)

    def body(kk, carry):
        k0 = 2 * kk
        scores(k0 + 1, buf_b)
        carry = softmax_pv(k0, buf_a, carry, False)
        scores(k0 + 2, buf_a)
        return softmax_pv(k0 + 1, buf_b, carry, False)

    def last_even(carry):
        return softmax_pv(qi, buf_a, carry, True)

    def last_odd(carry):
        scores(qi, buf_b)
        return softmax_pv(qi, buf_b, softmax_pv(qi - 1, buf_a, carry, False), True)

    neg = jnp.full((1, TQ), NEG_INF, F32)
    zero_l = jnp.zeros((1, TQ), F32)
    scores(0, buf_a)
    carry = lax.fori_loop(0, qi // 2, body, ((neg, zero_l), (neg, zero_l)))
    carry = lax.cond(qi % 2 == 0, last_even, last_odd, carry)

    ot = acc_s[0] / carry[0][1] - lam * (acc_s[1] / carry[1][1])
    o = ot.T
    o_ref[...] = _rms_rows(o, subg_ref[...]) * (1.0 - lam_init)


def _attn_core(page_table, q, kb, vt, qd, knd, vnd, cache_kt, cache_vf, layer,
               lq1, lk1, lq2, lk2, subg, *, batch, seq, lam_init):
    n, n_pages = page_table.shape
    nq = seq // TQ
    nk = seq // TK
    assert batch * N_HEADS * nq == 2 * n

    def lin(b, h, i):
        return (b * N_HEADS + h) * nq + i

    vec = lambda w: pl.BlockSpec((1, w), lambda b, h, i, pt: (0, 0))
    row = pl.BlockSpec((1, 1, D_MODEL), lambda b, h, i, pt: (lin(b, h, i) // 2, 0, 0))
    per_head = pl.BlockSpec((1, N_HEADS, V_HEAD_DIM),
                            lambda b, h, i, pt: (lin(b, h, i) // 2, 0, 0))

    in_hbm = pl.BlockSpec(memory_space=pl.ANY)
    key = jnp.arange(PAGE_SIZE)
    expand = (key[:, None] == (jnp.arange(D_MODEL) // N_HEADS)[None, :]).astype(BF16)
    in_specs = ([vec(HEAD_DIM)] * 4 + [vec(V_HEAD_DIM)]
                + [pl.BlockSpec((TQ, V_HEAD_DIM), lambda b, h, i, pt: (b * nq + i, h)),
                   pl.BlockSpec((seq, V_HEAD_DIM), lambda b, h, i, pt: (b, h)),
                   pl.BlockSpec((1, nk, V_HEAD_DIM, TK), lambda b, h, i, pt: (h, b, 0, 0)),
                   row, row, per_head,
                   pl.BlockSpec((PAGE_SIZE, D_MODEL), lambda b, h, i, pt: (0, 0)),
                   in_hbm, in_hbm])
    o, od = pl.pallas_call(
        functools.partial(_attn_core_kernel, layer=layer, n_pages=n_pages, lam_init=lam_init),
        out_shape=(jax.ShapeDtypeStruct((batch * seq, N_HEADS * V_HEAD_DIM), F32),
                   jax.ShapeDtypeStruct((n, N_HEADS, V_HEAD_DIM), F32)),
        grid_spec=pltpu.PrefetchScalarGridSpec(
            num_scalar_prefetch=1,
            grid=(batch, N_HEADS, nq),
            in_specs=in_specs,
            out_specs=(pl.BlockSpec((TQ, V_HEAD_DIM), lambda b, h, i, pt: (b * nq + i, h)),
                       per_head),
            scratch_shapes=[pltpu.VMEM((2, TQ, V_HEAD_DIM), BF16),
                            pltpu.VMEM((2, TQ, TQ), F32),
                            pltpu.VMEM((2, TQ, TQ), F32),
                            pltpu.VMEM((2, 1, TQ), F32),
                            pltpu.VMEM((2, 1, TQ), F32),
                            pltpu.VMEM((2, V_HEAD_DIM, TQ), F32),
                            pltpu.VMEM((N_HEADS, n_pages * PAGE_SIZE), F32),
                            pltpu.VMEM((N_HEADS, V_HEAD_DIM), F32),
                            pltpu.VMEM((2, n_pages, D_MODEL, PAGE_SIZE), F32),
                            pltpu.SemaphoreType.DMA((2,))],
        ),
        compiler_params=_params("arbitrary", "arbitrary", "arbitrary"),
        name="attn_core",
    )(page_table, lq1, lk1, lq2, lk2, subg, q, kb, vt,
      qd.reshape(n, 1, D_MODEL), knd.reshape(n, 1, D_MODEL), vnd, expand, cache_kt, cache_vf)
    return o, od.reshape(n, D_MODEL)


def _decode_keys(lam, q_ref, kn_ref, k_refs, a_s, aown_s):
    nseg = 2 * N_HEADS
    q = q_ref[0].astype(F32)
    seg = lax.broadcasted_iota(jnp.int32, (nseg, D_MODEL), 1) // HEAD_DIM
    rowi = lax.broadcasted_iota(jnp.int32, (nseg, D_MODEL), 0)
    row_seg = 2 * (rowi % N_HEADS) + rowi // N_HEADS
    qbd = jnp.where(seg == row_seg, jnp.broadcast_to(q, (nseg, D_MODEL)), 0.0)
    qbd16 = qbd.astype(BF16)
    s = jnp.concatenate([_dot(qbd16, k_ref[...].astype(BF16)) for k_ref in k_refs],
                        axis=1)
    kn = kn_ref[0].astype(F32)
    s_own = jnp.sum(qbd * kn, axis=-1, keepdims=True)
    m = jnp.maximum(jnp.max(s, axis=-1, keepdims=True), s_own)
    p = jnp.exp2(s - m)
    p_own = jnp.exp2(s_own - m)
    inv_l = 1.0 / (jnp.sum(p, axis=-1, keepdims=True) + p_own)
    p = p * inv_l
    p_own = p_own * inv_l
    a_s[...] = p[:N_HEADS] - lam * p[N_HEADS:]
    aown_s[...] = jnp.broadcast_to(p_own[:N_HEADS] - lam * p_own[N_HEADS:], aown_s.shape)


def _decode_values(subg_ref, vn_ref, expand_ref, v_refs, a_s, aown_s, o_ref, *, lam_init):
    n_pages = len(v_refs)
    a = a_s[...]
    a_rows = jnp.concatenate([a[:, i * PAGE_SIZE:(i + 1) * PAGE_SIZE] for i in range(n_pages)],
                             axis=0).astype(BF16)
    w = _dot(a_rows, expand_ref[...])
    same_head = (lax.broadcasted_iota(jnp.int32, w.shape, 0) % N_HEADS
                 == lax.broadcasted_iota(jnp.int32, w.shape, 1) % N_HEADS)
    w = jnp.where(same_head, w, 0.0).astype(BF16)
    vn = vn_ref[0].astype(BF16).astype(F32)
    o = aown_s[...].astype(BF16).astype(F32) * vn
    for i, v_ref in enumerate(v_refs):
        o = o + _dot(w[i * N_HEADS:(i + 1) * N_HEADS], v_ref[...].astype(BF16))
    o_ref[0] = _rms_rows(o, subg_ref[...]) * (1.0 - lam_init)


def _attn_out_kernel(x_ref, o_ref, z_ref, w_ref, y_ref):
    gated = (o_ref[...] * _silu(z_ref[...])).astype(BF16)
    y_ref[...] = x_ref[...] + _dot(gated, w_ref[...])


def _attn_out(x2d, o, z, w_out, *, tm):
    r, d = x2d.shape
    row = pl.BlockSpec((tm, d), lambda i: (i, 0))
    return pl.pallas_call(
        _attn_out_kernel,
        out_shape=jax.ShapeDtypeStruct((r, d), F32),
        grid=(r // tm,),
        in_specs=[row, row, row, pl.BlockSpec((d, d), lambda i: (0, 0))],
        out_specs=row,
        compiler_params=_params("arbitrary"),
        name="attn_out",
    )(x2d, o, z, w_out)


def _rope_tables(pos):
    half = HEAD_DIM // 2
    inv = jnp.power(ROPE_THETA, -jnp.arange(0, HEAD_DIM, 2, dtype=F32) / HEAD_DIM)
    ang = pos.astype(F32)[:, None] * inv[None, :]
    cos = jnp.cos(ang)
    sin = jnp.sin(ang)
    cos_full = jnp.tile(cos, (1, V_HEAD_DIM // half))
    sin_full = jnp.tile(jnp.concatenate([-sin, sin], axis=-1), (1, V_HEAD_DIM // HEAD_DIM))
    return cos_full, sin_full


def kernel(x_prompt, x_sample, state_pool, cache_k, cache_v, page_table, norm_pool, w_in_pool, w_grp_pool, scale_pool, w_out_pool, norm_attn, w_in_attn, q_norm, k_norm, lambda_q1, lambda_k1, lambda_q2, lambda_k2, subln, w_out_attn):
    batch, seq, d = x_prompt.shape
    n_dec = x_sample.shape[0]
    n_attn = norm_attn.shape[0]
    depth = norm_pool.shape[0] + n_attn
    past_len = page_table.shape[1] * PAGE_SIZE
    n_phys = cache_k.shape[1]

    cos_p, sin_p = _rope_tables(jnp.arange(seq))
    cos_s, sin_s = _rope_tables(jnp.full((n_dec,), past_len, jnp.int32))
    seg = jnp.arange(256) // HEAD_DIM
    ones_bd = (seg[:, None] == seg[None, :]).astype(BF16)
    n_layers = cache_k.shape[0]
    cache_kt = jnp.transpose(cache_k, (0, 1, 3, 4, 5, 2)).reshape(n_layers, n_phys, d, PAGE_SIZE)
    cache_vf = cache_v.reshape(n_layers, n_phys, PAGE_SIZE * N_HEADS, V_HEAD_DIM)
    state_t = jnp.transpose(state_pool, (0, 2, 1, 3))

    xp = x_prompt
    xs = x_sample.reshape(n_dec, d)
    ps_prompt, ps_sample, kp, vp, ksm, vsm = [], [], [], [], [], []
    for i in range(depth):
        j = i // 2
        if i % 2 == 0:
            g = norm_pool[j].reshape(1, d)
            w_in = w_in_pool[j].astype(BF16)
            w_grp = w_grp_pool[j].astype(BF16)
            scale = scale_pool[j].reshape(1, POOL_WIDTH)
            w_out = w_out_pool[j].astype(BF16)
            xp, st_p = _pool_prompt(xp, g, w_in, w_grp, scale, w_out)
            xs, st_s = _pool_sample(xs, state_t, j, g, w_in, w_grp, scale, w_out)
            ps_prompt.append(st_p)
            ps_sample.append(st_s)
        else:
            lam_init = 0.8 - 0.6 * math.exp(-0.3 * i)
            g = norm_attn[j].reshape(1, d)
            w_in = w_in_attn[j].astype(BF16)
            w_out = w_out_attn[j].astype(BF16)
            qn = jnp.tile(q_norm[j], d // HEAD_DIM).reshape(1, d)
            kn = jnp.tile(k_norm[j], d // HEAD_DIM).reshape(1, d)
            lams = [a[j].reshape(1, HEAD_DIM) for a in (lambda_q1, lambda_k1, lambda_q2, lambda_k2)]
            subg = subln[j].reshape(1, V_HEAD_DIM)

            x2 = xp.reshape(batch * seq, d)
            last = j == n_attn - 1
            prev_p = (kp, vp) if last else ((), ())
            prev_s = (ksm, vsm) if last else ((), ())
            q, kb, kt, v, vt, z = _attn_proj(x2, g, w_in, ones_bd, qn, kn, cos_p, sin_p, *prev_p,
                                             tm=TM_PROJ, pos_tiles=seq // TM_PROJ)
            qs, kb_s, kt_s, v_s, _, z_s = _attn_proj(xs, g, w_in, ones_bd, qn, kn, cos_s, sin_s,
                                                     *prev_s, tm=n_dec, pos_tiles=1)
            o, o_s = _attn_core(page_table, q, kb, vt, qs, kb_s, v_s[-1], cache_kt, cache_vf, j,
                                *lams, subg, batch=batch, seq=seq, lam_init=lam_init)
            xp = _attn_out(x2, o, z, w_out, tm=512).reshape(batch, seq, d)
            xs = _attn_out(xs, o_s, z_s, w_out, tm=n_dec)
            kp.append(kt)
            vp.append(v)
            ksm.append(kt_s)
            vsm.append(v_s)
    ps_sample = jnp.transpose(jnp.concatenate(ps_sample, axis=0), (0, 2, 1, 3))
    k_prompt = jnp.transpose(kp[-1].reshape(n_attn, batch, N_HEADS, 2, HEAD_DIM, seq),
                             (0, 1, 5, 2, 3, 4))
    k_sample = jnp.transpose(ksm[-1].reshape(n_attn, 1, N_HEADS, 2, HEAD_DIM, n_dec),
                             (0, 5, 1, 2, 3, 4))
    v_prompt = vp[-1].reshape(n_attn, batch, seq, N_HEADS, V_HEAD_DIM)
    v_sample = vsm[-1].reshape(n_attn, n_dec, 1, N_HEADS, V_HEAD_DIM)
    return (xp, xs.reshape(n_dec, 1, d), jnp.stack(ps_prompt), ps_sample,
            k_prompt, v_prompt, k_sample, v_sample)
```

```python
import functools
import math

import jax
import jax.numpy as jnp
from jax import lax
from jax.experimental import pallas as pl
from jax.experimental.pallas import tpu as pltpu

D_MODEL = 1024
POOL_WIDTH = 2048
POOL_WINDOWS = (2, 4, 8, 16)
POOL_GROUP_WIDTH = 512
POOL_STATE_LEN = 15
N_HEADS = 8
HEAD_DIM = 64
V_HEAD_DIM = 128
PAGE_SIZE = 128
ROPE_THETA = 10000.0
RMS_EPS = 1e-6
NEG_INF = -1e30

VMEM_LIMIT_BYTES = 56 * 1024 * 1024
HALO = 16
TM_PROJ = 256
TQ = 512
TK = TM_PROJ
HEADS_PER_STEP = 2

F32 = jnp.float32
BF16 = jnp.bfloat16


def _params(*sem):
    return pltpu.CompilerParams(dimension_semantics=sem, vmem_limit_bytes=VMEM_LIMIT_BYTES)


def _rms_rows(x, g):
    ms = jnp.mean(x * x, axis=-1, keepdims=True)
    return x * lax.rsqrt(ms + RMS_EPS) * g


def _silu(z):
    return z * jax.nn.sigmoid(z)


def _dot(a, b):
    return jnp.dot(a, b, preferred_element_type=F32)


def _pool_prompt_kernel(x_ref, g_ref, win_ref, wgrp_ref, scale_ref, wout_ref,
                        y_ref, st_ref, ubuf, *, tm):
    t = pl.program_id(1)

    @pl.when(t == 0)
    def _():
        ubuf[0:HALO, :] = jnp.zeros((HALO, POOL_WIDTH), F32)

    x = x_ref[0]
    h = _rms_rows(x, g_ref[...]).astype(BF16)
    uz = _dot(h, win_ref[...])
    u = uz[:, :POOL_WIDTH]
    z = uz[:, POOL_WIDTH:]
    ubuf[HALO:HALO + tm, :] = u

    pos = t * tm + lax.broadcasted_iota(jnp.int32, (tm, 1), 0)
    ms = []
    for gi, w in enumerate(POOL_WINDOWS):
        cols = slice(gi * POOL_GROUP_WIDTH, (gi + 1) * POOL_GROUP_WIDTH)
        acc = ubuf[HALO:HALO + tm, cols]
        for i in range(1, w):
            acc = acc + ubuf[HALO - i:HALO - i + tm, cols]
        inv_cnt = 1.0 / jnp.minimum(pos + 1, w).astype(F32)
        r = acc * inv_cnt - u[:, cols]
        ms.append(_dot(r.astype(BF16), wgrp_ref[gi]))
    m = jnp.concatenate(ms, axis=-1) * scale_ref[...]
    gated = (m * _silu(z)).astype(BF16)
    y_ref[0] = x + _dot(gated, wout_ref[...])

    @pl.when(t == pl.num_programs(1) - 1)
    def _():
        st_ref[0] = ubuf[HALO + tm - POOL_STATE_LEN:HALO + tm, :]

    ubuf[0:HALO, :] = ubuf[tm:tm + HALO, :]


def _pool_prompt(x, g, w_in, w_grp, scale, w_out, tm=TM_PROJ):
    b, s, d = x.shape
    const = lambda *shape: pl.BlockSpec(shape, lambda i, j: (0,) * len(shape))
    return pl.pallas_call(
        functools.partial(_pool_prompt_kernel, tm=tm),
        out_shape=(jax.ShapeDtypeStruct((b, s, d), F32),
                   jax.ShapeDtypeStruct((b, POOL_STATE_LEN, POOL_WIDTH), F32)),
        grid=(b, s // tm),
        in_specs=[
            pl.BlockSpec((1, tm, d), lambda i, j: (i, j, 0)),
            const(1, d),
            const(d, 2 * POOL_WIDTH),
            const(len(POOL_WINDOWS), POOL_GROUP_WIDTH, POOL_GROUP_WIDTH),
            const(1, POOL_WIDTH),
            const(POOL_WIDTH, d),
        ],
        out_specs=(pl.BlockSpec((1, tm, d), lambda i, j: (i, j, 0)),
                   pl.BlockSpec((1, POOL_STATE_LEN, POOL_WIDTH), lambda i, j: (i, 0, 0))),
        scratch_shapes=[pltpu.VMEM((tm + HALO, POOL_WIDTH), F32)],
        compiler_params=_params("arbitrary", "arbitrary"),
        name="pool_prompt",
    )(x, g, w_in, w_grp, scale, w_out)


def _pool_sample_kernel(x_ref, st_ref, g_ref, win_ref, wgrp_ref, scale_ref, wout_ref,
                        y_ref, nst_ref):
    x = x_ref[...]
    h = _rms_rows(x, g_ref[...]).astype(BF16)
    uz = _dot(h, win_ref[...])
    u = uz[:, :POOL_WIDTH]
    z = uz[:, POOL_WIDTH:]
    nst_ref[0, 0:POOL_STATE_LEN - 1] = st_ref[0, 1:POOL_STATE_LEN]
    nst_ref[0, POOL_STATE_LEN - 1] = u
    ms = []
    for gi, w in enumerate(POOL_WINDOWS):
        cols = slice(gi * POOL_GROUP_WIDTH, (gi + 1) * POOL_GROUP_WIDTH)
        acc = u[:, cols]
        for i in range(1, w):
            acc = acc + st_ref[0, POOL_STATE_LEN - i, :, cols]
        r = acc * (1.0 / w) - u[:, cols]
        ms.append(_dot(r.astype(BF16), wgrp_ref[gi]))
    m = jnp.concatenate(ms, axis=-1) * scale_ref[...]
    gated = (m * _silu(z)).astype(BF16)
    y_ref[...] = x + _dot(gated, wout_ref[...])


def _pool_sample(x, state_t, layer, g, w_in, w_grp, scale, w_out, tb=32):
    n, d = x.shape
    const = lambda *shape: pl.BlockSpec(shape, lambda i: (0,) * len(shape))
    return pl.pallas_call(
        _pool_sample_kernel,
        out_shape=(jax.ShapeDtypeStruct((n, d), F32),
                   jax.ShapeDtypeStruct((1, POOL_STATE_LEN, n, POOL_WIDTH), F32)),
        grid=(n // tb,),
        in_specs=[
            pl.BlockSpec((tb, d), lambda i: (i, 0)),
            pl.BlockSpec((1, POOL_STATE_LEN, tb, POOL_WIDTH), lambda i: (layer, 0, i, 0)),
            const(1, d),
            const(d, 2 * POOL_WIDTH),
            const(len(POOL_WINDOWS), POOL_GROUP_WIDTH, POOL_GROUP_WIDTH),
            const(1, POOL_WIDTH),
            const(POOL_WIDTH, d),
        ],
        out_specs=(pl.BlockSpec((tb, d), lambda i: (i, 0)),
                   pl.BlockSpec((1, POOL_STATE_LEN, tb, POOL_WIDTH), lambda i: (0, 0, i, 0))),
        compiler_params=_params("arbitrary"),
        name="pool_sample",
    )(x, state_t, g, w_in, w_grp, scale, w_out)


def _seg_norm_rope(a, ones_ref, gain, cos, sin):
    sq = a * a
    hi = sq.astype(BF16)
    lo = (sq - hi.astype(F32)).astype(BF16)
    parts = []
    for c in range(a.shape[1] // 256):
        cs = slice(c * 256, (c + 1) * 256)
        parts.append(_dot(hi[:, cs], ones_ref[...]) + _dot(lo[:, cs], ones_ref[...]))
    ss = jnp.concatenate(parts, axis=-1)
    an = a * lax.rsqrt(ss * (1.0 / HEAD_DIM) + RMS_EPS) * gain
    lane = lax.broadcasted_iota(jnp.int32, (a.shape[0], V_HEAD_DIM), 1)
    first_half = (lane % HEAD_DIM) < (HEAD_DIM // 2)
    outs = []
    for hh in range(N_HEADS):
        xh = an[:, hh * V_HEAD_DIM:(hh + 1) * V_HEAD_DIM]
        swapped = jnp.where(first_half,
                            pltpu.roll(xh, V_HEAD_DIM - HEAD_DIM // 2, 1),
                            pltpu.roll(xh, HEAD_DIM // 2, 1))
        outs.append(xh * cos + swapped * sin)
    return jnp.concatenate(outs, axis=-1)


def _attn_proj_kernel(x_ref, g_ref, win_ref, ones_ref, qn_ref, kn_ref, cos_ref, sin_ref, *rest,
                      n_prev):
    prev_kt = rest[:n_prev]
    prev_v = rest[n_prev:2 * n_prev]
    q_ref, kb_ref, kt_ref, v_ref, vt_ref, z_ref = rest[2 * n_prev:]
    for j in range(n_prev):
        kt_ref[j] = prev_kt[j][0]
        v_ref[j] = prev_v[j][0]
    x = x_ref[...]
    h = _rms_rows(x, g_ref[...]).astype(BF16)
    proj = _dot(h, win_ref[...])
    cos = cos_ref[...]
    sin = sin_ref[...]
    q = _seg_norm_rope(proj[:, 0:D_MODEL], ones_ref, qn_ref[...], cos, sin)
    k = _seg_norm_rope(proj[:, D_MODEL:2 * D_MODEL], ones_ref, kn_ref[...], cos, sin)
    v = proj[:, 2 * D_MODEL:3 * D_MODEL]
    q_ref[...] = (q * (HEAD_DIM ** -0.5 * math.log2(math.e))).astype(BF16)
    kb_ref[...] = k.astype(BF16)
    kt_ref[n_prev, 0] = k.T
    v_ref[n_prev] = v.reshape(v.shape[0], N_HEADS, V_HEAD_DIM)
    vt_ref[:, 0] = v.T.astype(BF16).reshape(N_HEADS, V_HEAD_DIM, v.shape[0])
    z_ref[...] = proj[:, 3 * D_MODEL:]


def _attn_proj(x2d, g, w_in, ones, qn_full, kn_full, cos, sin, prev_kt=(), prev_v=(), *,
               tm, pos_tiles):
    r, d = x2d.shape
    nt = r // tm
    n_prev = len(prev_kt)
    const = lambda *shape: pl.BlockSpec(shape, lambda i: (0,) * len(shape))
    row = lambda width: pl.BlockSpec((tm, width), lambda i: (i, 0))
    tab = pl.BlockSpec((tm, V_HEAD_DIM), lambda i: (i % pos_tiles, 0))
    kt_spec = lambda layers: pl.BlockSpec((layers, 1, d, tm),
                                          lambda i: (0, i // pos_tiles, 0, i % pos_tiles))
    v_spec = lambda layers: pl.BlockSpec((layers, tm, N_HEADS, V_HEAD_DIM), lambda i: (0, i, 0, 0))
    return pl.pallas_call(
        functools.partial(_attn_proj_kernel, n_prev=n_prev),
        out_shape=(jax.ShapeDtypeStruct((r, d), BF16),
                   jax.ShapeDtypeStruct((r, d), BF16),
                   jax.ShapeDtypeStruct((n_prev + 1, nt // pos_tiles, d, pos_tiles * tm), F32),
                   jax.ShapeDtypeStruct((n_prev + 1, r, N_HEADS, V_HEAD_DIM), F32),
                   jax.ShapeDtypeStruct((N_HEADS, nt, V_HEAD_DIM, tm), BF16),
                   jax.ShapeDtypeStruct((r, d), F32)),
        grid=(nt,),
        in_specs=([row(d), const(1, d), const(d, 4 * d), const(256, 256),
                   const(1, d), const(1, d), tab, tab]
                  + [kt_spec(1)] * n_prev + [v_spec(1)] * n_prev),
        out_specs=(row(d), row(d), kt_spec(n_prev + 1), v_spec(n_prev + 1),
                   pl.BlockSpec((N_HEADS, 1, V_HEAD_DIM, tm), lambda i: (0, i, 0, 0)),
                   row(d)),
        compiler_params=_params("arbitrary"),
        name="attn_proj",
    )(x2d, g, w_in, ones, qn_full, kn_full, cos, sin, *prev_kt, *prev_v)


def _lambda(lq1, lk1, lq2, lk2, lam_init):
    a = jnp.sum(lq1 * lk1, axis=-1, keepdims=True)
    b = jnp.sum(lq2 * lk2, axis=-1, keepdims=True)
    return jnp.exp(a) - jnp.exp(b) + lam_init


def _attn_core_kernel(pt_ref, lq1_ref, lk1_ref, lq2_ref, lk2_ref, subg_ref, q_ref, k_ref, vt_ref,
                      qd_ref, knd_ref, vnd_ref, expand_ref, kcache_ref, vcache_ref,
                      o_ref, od_ref, qc_s, s_a, s_b, mx_a, mx_b, acc_s, pages, sem,
                      *, layer, n_pages, lam_init):
    lam = _lambda(lq1_ref[...], lk1_ref[...], lq2_ref[...], lk2_ref[...], lam_init)
    n_steps = pl.num_programs(0) * pl.num_programs(1) * pl.num_programs(2)
    step = (pl.program_id(0) * pl.num_programs(1) + pl.program_id(1)) * pl.num_programs(2) \
        + pl.program_id(2)
    key_pages = [pages.at[0, i] for i in range(n_pages)]
    value_pages = [pages.at[1, i] for i in range(n_pages)]

    def key_copy(r, i):
        return pltpu.make_async_copy(kcache_ref.at[layer, pt_ref[r, i]], key_pages[i], sem.at[0])

    def value_copy(r, i):
        return pltpu.make_async_copy(vcache_ref.at[layer, pt_ref[r, i]], value_pages[i], sem.at[1])

    @pl.when(step == 0)
    def _():
        for i in range(n_pages):
            key_copy(0, i).start()
        for i in range(n_pages):
            value_copy(0, i).start()

    for i in range(n_pages):
        key_copy(step, i).wait()
    a, a_own = _decode_keys(lam, qd_ref, knd_ref, key_pages)

    @pl.when(step + 1 < n_steps)
    def _():
        for i in range(n_pages):
            key_copy(step + 1, i).start()

    for i in range(n_pages):
        value_copy(step, i).wait()
    _decode_values(subg_ref, vnd_ref, expand_ref, value_pages, a, a_own, od_ref, lam_init=lam_init)

    @pl.when(step + 1 < n_steps)
    def _():
        for i in range(n_pages):
            value_copy(step + 1, i).start()

    _flash_tile(lam, subg_ref, q_ref, k_ref, vt_ref, o_ref, qc_s, s_a, s_b, mx_a, mx_b, acc_s,
                lam_init=lam_init)


def _flash_tile(lam, subg_ref, q_ref, k_ref, vt_ref, o_ref, qc_s, s_a, s_b, mx_a, mx_b, acc_s,
                *, lam_init):
    qi = pl.program_id(2)
    n_streams = 2 * HEADS_PER_STEP
    lane = lax.broadcasted_iota(jnp.int32, (TQ, V_HEAD_DIM), 1)
    for hl in range(HEADS_PER_STEP):
        q = q_ref[:, hl * V_HEAD_DIM:(hl + 1) * V_HEAD_DIM]
        zero = jnp.zeros_like(q)
        qc_s[2 * hl] = jnp.where(lane < HEAD_DIM, q, zero)
        qc_s[2 * hl + 1] = jnp.where(lane >= HEAD_DIM, q, zero)
    acc_s[...] = jnp.zeros(acc_s.shape, F32)

    def scores(ki, sm_ref):
        s_ref, mx_ref = sm_ref
        for hl in range(HEADS_PER_STEP):
            k = k_ref[pl.ds(pl.multiple_of(ki * TQ, TQ), TQ), hl * V_HEAD_DIM:(hl + 1) * V_HEAD_DIM]
            for st in (2 * hl, 2 * hl + 1):
                s = lax.dot_general(k, qc_s[st], (((1,), (1,)), ((), ())),
                                    preferred_element_type=F32)
                s_ref[st] = s
                mx_ref[st] = jnp.max(s, axis=0, keepdims=True)

    def softmax_pv(ki, sm_ref, carry, masked):
        s_ref, mx_ref = sm_ref
        if masked:
            keep = (lax.broadcasted_iota(jnp.int32, (TQ, TQ), 0)
                    <= lax.broadcasted_iota(jnp.int32, (TQ, TQ), 1))
        out = []
        for st in range(n_streams):
            m_old, l_old = carry[st]
            s = s_ref[st]
            if masked:
                s = jnp.where(keep, s, NEG_INF)
                m_tile = jnp.max(s, axis=0, keepdims=True)
            else:
                m_tile = mx_ref[st]
            m_new = jnp.maximum(m_old, m_tile)
            alpha = jnp.exp2(m_old - m_new)
            p = jnp.exp2(s - m_new)
            l_new = alpha * l_old + jnp.sum(p, axis=0, keepdims=True)
            pb = p.astype(BF16)
            sub = TQ // TK
            pv = _dot(vt_ref[st // 2, sub * ki], pb[0:TK])
            for j in range(1, sub):
                pv = pv + _dot(vt_ref[st // 2, sub * ki + j], pb[j * TK:(j + 1) * TK])
            acc_s[st] = alpha * acc_s[st] + pv
            out.append((m_new, l_new))
        return tuple(out)

    buf_a = (s_a, mx_a)
    buf_b = (s_b, mx_b)

    def body(kk, carry):
        k0 = 2 * kk
        scores(k0 + 1, buf_b)
        carry = softmax_pv(k0, buf_a, carry, False)
        scores(k0 + 2, buf_a)
        return softmax_pv(k0 + 1, buf_b, carry, False)

    def last_even(carry):
        return softmax_pv(qi, buf_a, carry, True)

    def last_odd(carry):
        scores(qi, buf_b)
        return softmax_pv(qi, buf_b, softmax_pv(qi - 1, buf_a, carry, False), True)

    neg = jnp.full((1, TQ), NEG_INF, F32)
    zero_l = jnp.zeros((1, TQ), F32)
    scores(0, buf_a)
    carry = lax.fori_loop(0, qi // 2, body, ((neg, zero_l),) * n_streams)
    carry = lax.cond(qi % 2 == 0, last_even, last_odd, carry)

    for hl in range(HEADS_PER_STEP):
        ot = (acc_s[2 * hl] / carry[2 * hl][1]
              - lam * (acc_s[2 * hl + 1] / carry[2 * hl + 1][1]))
        o = ot.T
        o_ref[:, hl * V_HEAD_DIM:(hl + 1) * V_HEAD_DIM] = (
            _rms_rows(o, subg_ref[...]) * (1.0 - lam_init))


def _attn_core(page_table, q, kb, vt, qd, knd, vnd, cache_kt, cache_vf, layer,
               lq1, lk1, lq2, lk2, subg, *, batch, seq, lam_init):
    n, n_pages = page_table.shape
    nq = seq // TQ
    nk = seq // TK
    hps = HEADS_PER_STEP
    groups = N_HEADS // hps
    assert batch * groups * nq == n

    def lin(b, h, i):
        return (b * groups + h) * nq + i

    vec = lambda w: pl.BlockSpec((1, w), lambda b, h, i, pt: (0, 0))
    row = pl.BlockSpec((1, 1, D_MODEL), lambda b, h, i, pt: (lin(b, h, i), 0, 0))
    per_head = pl.BlockSpec((1, N_HEADS, V_HEAD_DIM), lambda b, h, i, pt: (lin(b, h, i), 0, 0))

    in_hbm = pl.BlockSpec(memory_space=pl.ANY)
    key = jnp.arange(PAGE_SIZE)
    expand = (key[:, None] == (jnp.arange(D_MODEL) // N_HEADS)[None, :]).astype(BF16)
    in_specs = ([vec(HEAD_DIM)] * 4 + [vec(V_HEAD_DIM)]
                + [pl.BlockSpec((TQ, hps * V_HEAD_DIM), lambda b, h, i, pt: (b * nq + i, h)),
                   pl.BlockSpec((seq, hps * V_HEAD_DIM), lambda b, h, i, pt: (b, h)),
                   pl.BlockSpec((hps, nk, V_HEAD_DIM, TK), lambda b, h, i, pt: (h, b, 0, 0)),
                   row, row, per_head,
                   pl.BlockSpec((PAGE_SIZE, D_MODEL), lambda b, h, i, pt: (0, 0)),
                   in_hbm, in_hbm])
    o, od = pl.pallas_call(
        functools.partial(_attn_core_kernel, layer=layer, n_pages=n_pages, lam_init=lam_init),
        out_shape=(jax.ShapeDtypeStruct((batch * seq, N_HEADS * V_HEAD_DIM), F32),
                   jax.ShapeDtypeStruct((n, N_HEADS, V_HEAD_DIM), F32)),
        grid_spec=pltpu.PrefetchScalarGridSpec(
            num_scalar_prefetch=1,
            grid=(batch, groups, nq),
            in_specs=in_specs,
            out_specs=(pl.BlockSpec((TQ, hps * V_HEAD_DIM), lambda b, h, i, pt: (b * nq + i, h)),
                       per_head),
            scratch_shapes=[pltpu.VMEM((2 * hps, TQ, V_HEAD_DIM), BF16),
                            pltpu.VMEM((2 * hps, TQ, TQ), F32),
                            pltpu.VMEM((2 * hps, TQ, TQ), F32),
                            pltpu.VMEM((2 * hps, 1, TQ), F32),
                            pltpu.VMEM((2 * hps, 1, TQ), F32),
                            pltpu.VMEM((2 * hps, V_HEAD_DIM, TQ), F32),
                            pltpu.VMEM((2, n_pages, D_MODEL, PAGE_SIZE), F32),
                            pltpu.SemaphoreType.DMA((2,))],
        ),
        compiler_params=_params("arbitrary", "arbitrary", "arbitrary"),
        name="attn_core",
    )(page_table, lq1, lk1, lq2, lk2, subg, q, kb, vt,
      qd.reshape(n, 1, D_MODEL), knd.reshape(n, 1, D_MODEL), vnd, expand, cache_kt, cache_vf)
    return o, od.reshape(n, D_MODEL)


def _decode_keys(lam, q_ref, kn_ref, k_refs):
    nseg = 2 * N_HEADS
    q = q_ref[0].astype(F32)
    seg = lax.broadcasted_iota(jnp.int32, (nseg, D_MODEL), 1) // HEAD_DIM
    rowi = lax.broadcasted_iota(jnp.int32, (nseg, D_MODEL), 0)
    row_seg = 2 * (rowi % N_HEADS) + rowi // N_HEADS
    qbd = jnp.where(seg == row_seg, jnp.broadcast_to(q, (nseg, D_MODEL)), 0.0)
    qbd16 = qbd.astype(BF16)
    s = jnp.concatenate([_dot(qbd16, k_ref[...].astype(BF16)) for k_ref in k_refs],
                        axis=1)
    kn = kn_ref[0].astype(F32)
    s_own = jnp.sum(qbd * kn, axis=-1, keepdims=True)
    m = jnp.maximum(jnp.max(s, axis=-1, keepdims=True), s_own)
    p = jnp.exp2(s - m)
    p_own = jnp.exp2(s_own - m)
    inv_l = 1.0 / (jnp.sum(p, axis=-1, keepdims=True) + p_own)
    p = p * inv_l
    p_own = p_own * inv_l
    return p[:N_HEADS] - lam * p[N_HEADS:], p_own[:N_HEADS] - lam * p_own[N_HEADS:]


def _decode_values(subg_ref, vn_ref, expand_ref, v_refs, a, a_own, o_ref, *, lam_init):
    n_pages = len(v_refs)
    a_rows = jnp.concatenate([a[:, i * PAGE_SIZE:(i + 1) * PAGE_SIZE] for i in range(n_pages)],
                             axis=0).astype(BF16)
    w = _dot(a_rows, expand_ref[...])
    same_head = (lax.broadcasted_iota(jnp.int32, w.shape, 0) % N_HEADS
                 == lax.broadcasted_iota(jnp.int32, w.shape, 1) % N_HEADS)
    w = jnp.where(same_head, w, 0.0).astype(BF16)
    vn = vn_ref[0].astype(BF16).astype(F32)
    o = a_own.astype(BF16).astype(F32) * vn
    for i, v_ref in enumerate(v_refs):
        o = o + _dot(w[i * N_HEADS:(i + 1) * N_HEADS], v_ref[...].astype(BF16))
    o_ref[0] = _rms_rows(o, subg_ref[...]) * (1.0 - lam_init)


def _attn_out_kernel(x_ref, o_ref, z_ref, w_ref, y_ref):
    gated = (o_ref[...] * _silu(z_ref[...])).astype(BF16)
    y_ref[...] = x_ref[...] + _dot(gated, w_ref[...])


def _attn_out(x2d, o, z, w_out, *, tm):
    r, d = x2d.shape
    row = pl.BlockSpec((tm, d), lambda i: (i, 0))
    return pl.pallas_call(
        _attn_out_kernel,
        out_shape=jax.ShapeDtypeStruct((r, d), F32),
        grid=(r // tm,),
        in_specs=[row, row, row, pl.BlockSpec((d, d), lambda i: (0, 0))],
        out_specs=row,
        compiler_params=_params("arbitrary"),
        name="attn_out",
    )(x2d, o, z, w_out)


def _rope_tables(pos):
    half = HEAD_DIM // 2
    inv = jnp.power(ROPE_THETA, -jnp.arange(0, HEAD_DIM, 2, dtype=F32) / HEAD_DIM)
    ang = pos.astype(F32)[:, None] * inv[None, :]
    cos = jnp.cos(ang)
    sin = jnp.sin(ang)
    cos_full = jnp.tile(cos, (1, V_HEAD_DIM // half))
    sin_full = jnp.tile(jnp.concatenate([-sin, sin], axis=-1), (1, V_HEAD_DIM // HEAD_DIM))
    return cos_full, sin_full


def kernel(x_prompt, x_sample, state_pool, cache_k, cache_v, page_table, norm_pool, w_in_pool, w_grp_pool, scale_pool, w_out_pool, norm_attn, w_in_attn, q_norm, k_norm, lambda_q1, lambda_k1, lambda_q2, lambda_k2, subln, w_out_attn):
    batch, seq, d = x_prompt.shape
    n_dec = x_sample.shape[0]
    n_attn = norm_attn.shape[0]
    depth = norm_pool.shape[0] + n_attn
    past_len = page_table.shape[1] * PAGE_SIZE
    n_phys = cache_k.shape[1]

    cos_p, sin_p = _rope_tables(jnp.arange(seq))
    cos_s, sin_s = _rope_tables(jnp.full((n_dec,), past_len, jnp.int32))
    seg = jnp.arange(256) // HEAD_DIM
    ones_bd = (seg[:, None] == seg[None, :]).astype(BF16)
    n_layers = cache_k.shape[0]
    cache_kt = jnp.transpose(cache_k, (0, 1, 3, 4, 5, 2)).reshape(n_layers, n_phys, d, PAGE_SIZE)
    cache_vf = cache_v.reshape(n_layers, n_phys, PAGE_SIZE * N_HEADS, V_HEAD_DIM)
    state_t = jnp.transpose(state_pool, (0, 2, 1, 3))

    xp = x_prompt
    xs = x_sample.reshape(n_dec, d)
    ps_prompt, ps_sample, kp, vp, ksm, vsm = [], [], [], [], [], []
    for i in range(depth):
        j = i // 2
        if i % 2 == 0:
            g = norm_pool[j].reshape(1, d)
            w_in = w_in_pool[j].astype(BF16)
            w_grp = w_grp_pool[j].astype(BF16)
            scale = scale_pool[j].reshape(1, POOL_WIDTH)
            w_out = w_out_pool[j].astype(BF16)
            xp, st_p = _pool_prompt(xp, g, w_in, w_grp, scale, w_out)
            xs, st_s = _pool_sample(xs, state_t, j, g, w_in, w_grp, scale, w_out)
            ps_prompt.append(st_p)
            ps_sample.append(st_s)
        else:
            lam_init = 0.8 - 0.6 * math.exp(-0.3 * i)
            g = norm_attn[j].reshape(1, d)
            w_in = w_in_attn[j].astype(BF16)
            w_out = w_out_attn[j].astype(BF16)
            qn = jnp.tile(q_norm[j], d // HEAD_DIM).reshape(1, d)
            kn = jnp.tile(k_norm[j], d // HEAD_DIM).reshape(1, d)
            lams = [a[j].reshape(1, HEAD_DIM) for a in (lambda_q1, lambda_k1, lambda_q2, lambda_k2)]
            subg = subln[j].reshape(1, V_HEAD_DIM)

            x2 = xp.reshape(batch * seq, d)
            last = j == n_attn - 1
            prev_p = (kp, vp) if last else ((), ())
            prev_s = (ksm, vsm) if last else ((), ())
            q, kb, kt, v, vt, z = _attn_proj(x2, g, w_in, ones_bd, qn, kn, cos_p, sin_p, *prev_p,
                                             tm=TM_PROJ, pos_tiles=seq // TM_PROJ)
            qs, kb_s, kt_s, v_s, _, z_s = _attn_proj(xs, g, w_in, ones_bd, qn, kn, cos_s, sin_s,
                                                     *prev_s, tm=n_dec, pos_tiles=1)
            o, o_s = _attn_core(page_table, q, kb, vt, qs, kb_s, v_s[-1], cache_kt, cache_vf, j,
                                *lams, subg, batch=batch, seq=seq, lam_init=lam_init)
            xp = _attn_out(x2, o, z, w_out, tm=512).reshape(batch, seq, d)
            xs = _attn_out(xs, o_s, z_s, w_out, tm=n_dec)
            kp.append(kt)
            vp.append(v)
            ksm.append(kt_s)
            vsm.append(v_s)
    ps_sample = jnp.transpose(jnp.concatenate(ps_sample, axis=0), (0, 2, 1, 3))
    k_prompt = jnp.transpose(kp[-1].reshape(n_attn, batch, N_HEADS, 2, HEAD_DIM, seq),
                             (0, 1, 5, 2, 3, 4))
    k_sample = jnp.transpose(ksm[-1].reshape(n_attn, 1, N_HEADS, 2, HEAD_DIM, n_dec),
                             (0, 5, 1, 2, 3, 4))
    v_prompt = vp[-1].reshape(n_attn, batch, seq, N_HEADS, V_HEAD_DIM)
    v_sample = vsm[-1].reshape(n_attn, n_dec, 1, N_HEADS, V_HEAD_DIM)
    return (xp, xs.reshape(n_dec, 1, d), jnp.stack(ps_prompt), ps_sample,
            k_prompt, v_prompt, k_sample, v_sample)
```

```python
import functools
import math

import jax
import jax.numpy as jnp
from jax import lax
from jax.experimental import pallas as pl
from jax.experimental.pallas import tpu as pltpu

D_MODEL = 1024
POOL_WIDTH = 2048
POOL_WINDOWS = (2, 4, 8, 16)
POOL_GROUP_WIDTH = 512
POOL_STATE_LEN = 15
N_HEADS = 8
HEAD_DIM = 64
V_HEAD_DIM = 128
PAGE_SIZE = 128
ROPE_THETA = 10000.0
RMS_EPS = 1e-6
NEG_INF = -1e30

VMEM_LIMIT_BYTES = 56 * 1024 * 1024
HALO = 16
TM_PROJ = 256
TM_POOL = 512
TQ = 512
TK = TM_PROJ
HEADS_PER_STEP = 2

F32 = jnp.float32
BF16 = jnp.bfloat16


def _params(*sem):
    return pltpu.CompilerParams(dimension_semantics=sem, vmem_limit_bytes=VMEM_LIMIT_BYTES)


def _rms_rows(x, g):
    ms = jnp.mean(x * x, axis=-1, keepdims=True)
    return x * lax.rsqrt(ms + RMS_EPS) * g


def _silu(z):
    return z * jax.nn.sigmoid(z)


def _dot(a, b):
    return jnp.dot(a, b, preferred_element_type=F32)


def _pool_prompt_kernel(x_ref, g_ref, win_ref, wgrp_ref, scale_ref, wout_ref,
                        y_ref, st_ref, ubuf, *, tm):
    t = pl.program_id(1)

    @pl.when(t == 0)
    def _():
        ubuf[0:HALO, :] = jnp.zeros((HALO, POOL_WIDTH), F32)

    x = x_ref[0]
    h = _rms_rows(x, g_ref[...]).astype(BF16)
    uz = _dot(h, win_ref[...])
    u = uz[:, :POOL_WIDTH]
    z = uz[:, POOL_WIDTH:]
    ubuf[HALO:HALO + tm, :] = u

    pos = t * tm + lax.broadcasted_iota(jnp.int32, (tm, 1), 0)
    ms = []
    for gi, w in enumerate(POOL_WINDOWS):
        cols = slice(gi * POOL_GROUP_WIDTH, (gi + 1) * POOL_GROUP_WIDTH)
        acc = ubuf[HALO:HALO + tm, cols]
        for i in range(1, w):
            acc = acc + ubuf[HALO - i:HALO - i + tm, cols]
        inv_cnt = 1.0 / jnp.minimum(pos + 1, w).astype(F32)
        r = acc * inv_cnt - u[:, cols]
        ms.append(_dot(r.astype(BF16), wgrp_ref[gi]))
    m = jnp.concatenate(ms, axis=-1) * scale_ref[...]
    gated = (m * _silu(z)).astype(BF16)
    y_ref[0] = x + _dot(gated, wout_ref[...])

    @pl.when(t == pl.num_programs(1) - 1)
    def _():
        st_ref[0] = ubuf[HALO + tm - POOL_STATE_LEN:HALO + tm, :]

    ubuf[0:HALO, :] = ubuf[tm:tm + HALO, :]


def _pool_prompt(x, g, w_in, w_grp, scale, w_out, tm=TM_POOL):
    b, s, d = x.shape
    const = lambda *shape: pl.BlockSpec(shape, lambda i, j: (0,) * len(shape),
                                        pipeline_mode=pl.Buffered(1))
    return pl.pallas_call(
        functools.partial(_pool_prompt_kernel, tm=tm),
        out_shape=(jax.ShapeDtypeStruct((b, s, d), F32),
                   jax.ShapeDtypeStruct((b, POOL_STATE_LEN, POOL_WIDTH), F32)),
        grid=(b, s // tm),
        in_specs=[
            pl.BlockSpec((1, tm, d), lambda i, j: (i, j, 0)),
            const(1, d),
            const(d, 2 * POOL_WIDTH),
            const(len(POOL_WINDOWS), POOL_GROUP_WIDTH, POOL_GROUP_WIDTH),
            const(1, POOL_WIDTH),
            const(POOL_WIDTH, d),
        ],
        out_specs=(pl.BlockSpec((1, tm, d), lambda i, j: (i, j, 0)),
                   pl.BlockSpec((1, POOL_STATE_LEN, POOL_WIDTH), lambda i, j: (i, 0, 0))),
        scratch_shapes=[pltpu.VMEM((tm + HALO, POOL_WIDTH), F32)],
        compiler_params=_params("arbitrary", "arbitrary"),
        name="pool_prompt",
    )(x, g, w_in, w_grp, scale, w_out)


def _pool_sample_kernel(x_ref, st_ref, g_ref, win_ref, wgrp_ref, scale_ref, wout_ref,
                        y_ref, nst_ref):
    x = x_ref[...]
    h = _rms_rows(x, g_ref[...]).astype(BF16)
    uz = _dot(h, win_ref[...])
    u = uz[:, :POOL_WIDTH]
    z = uz[:, POOL_WIDTH:]
    nst_ref[0, 0:POOL_STATE_LEN - 1] = st_ref[0, 1:POOL_STATE_LEN]
    nst_ref[0, POOL_STATE_LEN - 1] = u
    ms = []
    for gi, w in enumerate(POOL_WINDOWS):
        cols = slice(gi * POOL_GROUP_WIDTH, (gi + 1) * POOL_GROUP_WIDTH)
        acc = u[:, cols]
        for i in range(1, w):
            acc = acc + st_ref[0, POOL_STATE_LEN - i, :, cols]
        r = acc * (1.0 / w) - u[:, cols]
        ms.append(_dot(r.astype(BF16), wgrp_ref[gi]))
    m = jnp.concatenate(ms, axis=-1) * scale_ref[...]
    gated = (m * _silu(z)).astype(BF16)
    y_ref[...] = x + _dot(gated, wout_ref[...])


def _pool_sample(x, state_t, layer, g, w_in, w_grp, scale, w_out, tb=32):
    n, d = x.shape
    const = lambda *shape: pl.BlockSpec(shape, lambda i: (0,) * len(shape))
    return pl.pallas_call(
        _pool_sample_kernel,
        out_shape=(jax.ShapeDtypeStruct((n, d), F32),
                   jax.ShapeDtypeStruct((1, POOL_STATE_LEN, n, POOL_WIDTH), F32)),
        grid=(n // tb,),
        in_specs=[
            pl.BlockSpec((tb, d), lambda i: (i, 0)),
            pl.BlockSpec((1, POOL_STATE_LEN, tb, POOL_WIDTH), lambda i: (layer, 0, i, 0)),
            const(1, d),
            const(d, 2 * POOL_WIDTH),
            const(len(POOL_WINDOWS), POOL_GROUP_WIDTH, POOL_GROUP_WIDTH),
            const(1, POOL_WIDTH),
            const(POOL_WIDTH, d),
        ],
        out_specs=(pl.BlockSpec((tb, d), lambda i: (i, 0)),
                   pl.BlockSpec((1, POOL_STATE_LEN, tb, POOL_WIDTH), lambda i: (0, 0, i, 0))),
        compiler_params=_params("arbitrary"),
        name="pool_sample",
    )(x, state_t, g, w_in, w_grp, scale, w_out)


def _seg_norm_rope(a, ones_ref, gain, cos, sin):
    sq = a * a
    hi = sq.astype(BF16)
    lo = (sq - hi.astype(F32)).astype(BF16)
    parts = []
    for c in range(a.shape[1] // 256):
        cs = slice(c * 256, (c + 1) * 256)
        parts.append(_dot(hi[:, cs], ones_ref[...]) + _dot(lo[:, cs], ones_ref[...]))
    ss = jnp.concatenate(parts, axis=-1)
    an = a * lax.rsqrt(ss * (1.0 / HEAD_DIM) + RMS_EPS) * gain
    lane = lax.broadcasted_iota(jnp.int32, (a.shape[0], V_HEAD_DIM), 1)
    first_half = (lane % HEAD_DIM) < (HEAD_DIM // 2)
    outs = []
    for hh in range(N_HEADS):
        xh = an[:, hh * V_HEAD_DIM:(hh + 1) * V_HEAD_DIM]
        swapped = jnp.where(first_half,
                            pltpu.roll(xh, V_HEAD_DIM - HEAD_DIM // 2, 1),
                            pltpu.roll(xh, HEAD_DIM // 2, 1))
        outs.append(xh * cos + swapped * sin)
    return jnp.concatenate(outs, axis=-1)


def _attn_proj_kernel(x_ref, g_ref, win_ref, ones_ref, qn_ref, kn_ref, cos_ref, sin_ref, *rest,
                      n_prev):
    prev_kt = rest[:n_prev]
    prev_v = rest[n_prev:2 * n_prev]
    q_ref, kb_ref, kt_ref, v_ref, vt_ref, z_ref = rest[2 * n_prev:]
    for j in range(n_prev):
        kt_ref[j] = prev_kt[j][0]
        v_ref[j] = prev_v[j][0]
    x = x_ref[...]
    h = _rms_rows(x, g_ref[...]).astype(BF16)
    cos = cos_ref[...]
    sin = sin_ref[...]
    v = _dot(h, win_ref[:, 2 * D_MODEL:3 * D_MODEL])
    v_ref[n_prev] = v.reshape(v.shape[0], N_HEADS, V_HEAD_DIM)
    vt_ref[:, 0] = v.T.astype(BF16).reshape(N_HEADS, V_HEAD_DIM, v.shape[0])
    k = _seg_norm_rope(_dot(h, win_ref[:, D_MODEL:2 * D_MODEL]), ones_ref, kn_ref[...], cos, sin)
    kb_ref[...] = k.astype(BF16)
    kt_ref[n_prev, 0] = k.T
    q = _seg_norm_rope(_dot(h, win_ref[:, 0:D_MODEL]), ones_ref, qn_ref[...], cos, sin)
    q_ref[...] = (q * (HEAD_DIM ** -0.5 * math.log2(math.e))).astype(BF16)
    z_ref[...] = _dot(h, win_ref[:, 3 * D_MODEL:])


def _attn_proj(x2d, g, w_in, ones, qn_full, kn_full, cos, sin, prev_kt=(), prev_v=(), *,
               tm, pos_tiles):
    r, d = x2d.shape
    nt = r // tm
    n_prev = len(prev_kt)
    const = lambda *shape: pl.BlockSpec(shape, lambda i: (0,) * len(shape))
    row = lambda width: pl.BlockSpec((tm, width), lambda i: (i, 0))
    tab = pl.BlockSpec((tm, V_HEAD_DIM), lambda i: (i % pos_tiles, 0))
    kt_spec = lambda layers: pl.BlockSpec((layers, 1, d, tm),
                                          lambda i: (0, i // pos_tiles, 0, i % pos_tiles))
    v_spec = lambda layers: pl.BlockSpec((layers, tm, N_HEADS, V_HEAD_DIM), lambda i: (0, i, 0, 0))
    return pl.pallas_call(
        functools.partial(_attn_proj_kernel, n_prev=n_prev),
        out_shape=(jax.ShapeDtypeStruct((r, d), BF16),
                   jax.ShapeDtypeStruct((r, d), BF16),
                   jax.ShapeDtypeStruct((n_prev + 1, nt // pos_tiles, d, pos_tiles * tm), F32),
                   jax.ShapeDtypeStruct((n_prev + 1, r, N_HEADS, V_HEAD_DIM), F32),
                   jax.ShapeDtypeStruct((N_HEADS, nt, V_HEAD_DIM, tm), BF16),
                   jax.ShapeDtypeStruct((r, d), F32)),
        grid=(nt,),
        in_specs=([row(d), const(1, d), const(d, 4 * d), const(256, 256),
                   const(1, d), const(1, d), tab, tab]
                  + [kt_spec(1)] * n_prev + [v_spec(1)] * n_prev),
        out_specs=(row(d), row(d), kt_spec(n_prev + 1), v_spec(n_prev + 1),
                   pl.BlockSpec((N_HEADS, 1, V_HEAD_DIM, tm), lambda i: (0, i, 0, 0)),
                   row(d)),
        compiler_params=_params("arbitrary"),
        name="attn_proj",
    )(x2d, g, w_in, ones, qn_full, kn_full, cos, sin, *prev_kt, *prev_v)


def _lambda(lq1, lk1, lq2, lk2, lam_init):
    a = jnp.sum(lq1 * lk1, axis=-1, keepdims=True)
    b = jnp.sum(lq2 * lk2, axis=-1, keepdims=True)
    return jnp.exp(a) - jnp.exp(b) + lam_init


def _attn_core_kernel(pt_ref, lq1_ref, lk1_ref, lq2_ref, lk2_ref, subg_ref, q_ref, k_ref, vt_ref,
                      qd_ref, knd_ref, vnd_ref, expand_ref, kcache_ref, vcache_ref,
                      o_ref, od_ref, qc_s, s_a, s_b, mx_a, mx_b, acc_s, pages, sem,
                      *, layer, n_pages, lam_init):
    lam = _lambda(lq1_ref[...], lk1_ref[...], lq2_ref[...], lk2_ref[...], lam_init)
    n_steps = pl.num_programs(0) * pl.num_programs(1) * pl.num_programs(2)
    step = (pl.program_id(0) * pl.num_programs(1) + pl.program_id(1)) * pl.num_programs(2) \
        + pl.program_id(2)
    key_pages = [pages.at[0, i] for i in range(n_pages)]
    value_pages = [pages.at[1, i] for i in range(n_pages)]

    def key_copy(r, i):
        return pltpu.make_async_copy(kcache_ref.at[layer, pt_ref[r, i]], key_pages[i], sem.at[0])

    def value_copy(r, i):
        return pltpu.make_async_copy(vcache_ref.at[layer, pt_ref[r, i]], value_pages[i], sem.at[1])

    @pl.when(step == 0)
    def _():
        for i in range(n_pages):
            key_copy(0, i).start()
        for i in range(n_pages):
            value_copy(0, i).start()

    for i in range(n_pages):
        key_copy(step, i).wait()
    a, a_own = _decode_keys(lam, qd_ref, knd_ref, key_pages)

    @pl.when(step + 1 < n_steps)
    def _():
        for i in range(n_pages):
            key_copy(step + 1, i).start()

    for i in range(n_pages):
        value_copy(step, i).wait()
    _decode_values(subg_ref, vnd_ref, expand_ref, value_pages, a, a_own, od_ref, lam_init=lam_init)

    @pl.when(step + 1 < n_steps)
    def _():
        for i in range(n_pages):
            value_copy(step + 1, i).start()

    _flash_tile(lam, subg_ref, q_ref, k_ref, vt_ref, o_ref, qc_s, s_a, s_b, mx_a, mx_b, acc_s,
                lam_init=lam_init)


def _flash_tile(lam, subg_ref, q_ref, k_ref, vt_ref, o_ref, qc_s, s_a, s_b, mx_a, mx_b, acc_s,
                *, lam_init):
    qi = pl.program_id(2)
    n_streams = 2 * HEADS_PER_STEP
    lane = lax.broadcasted_iota(jnp.int32, (TQ, V_HEAD_DIM), 1)
    for hl in range(HEADS_PER_STEP):
        q = q_ref[:, hl * V_HEAD_DIM:(hl + 1) * V_HEAD_DIM]
        zero = jnp.zeros_like(q)
        qc_s[2 * hl] = jnp.where(lane < HEAD_DIM, q, zero)
        qc_s[2 * hl + 1] = jnp.where(lane >= HEAD_DIM, q, zero)
    acc_s[...] = jnp.zeros(acc_s.shape, F32)

    def scores(ki, sm_ref):
        s_ref, mx_ref = sm_ref
        for hl in range(HEADS_PER_STEP):
            k = k_ref[pl.ds(pl.multiple_of(ki * TQ, TQ), TQ), hl * V_HEAD_DIM:(hl + 1) * V_HEAD_DIM]
            for st in (2 * hl, 2 * hl + 1):
                s = lax.dot_general(k, qc_s[st], (((1,), (1,)), ((), ())),
                                    preferred_element_type=F32)
                s_ref[st] = s
                mx_ref[st] = jnp.max(s, axis=0, keepdims=True)

    def softmax_pv(ki, sm_ref, carry, masked):
        s_ref, mx_ref = sm_ref
        if masked:
            keep = (lax.broadcasted_iota(jnp.int32, (TQ, TQ), 0)
                    <= lax.broadcasted_iota(jnp.int32, (TQ, TQ), 1))
        out = []
        for st in range(n_streams):
            m_old, l_old = carry[st]
            s = s_ref[st]
            if masked:
                s = jnp.where(keep, s, NEG_INF)
                m_tile = jnp.max(s, axis=0, keepdims=True)
            else:
                m_tile = mx_ref[st]
            m_new = jnp.maximum(m_old, m_tile)
            alpha = jnp.exp2(m_old - m_new)
            p = jnp.exp2(s - m_new)
            l_new = alpha * l_old + jnp.sum(p, axis=0, keepdims=True)
            pb = p.astype(BF16)
            sub = TQ // TK
            pv = _dot(vt_ref[st // 2, sub * ki], pb[0:TK])
            for j in range(1, sub):
                pv = pv + _dot(vt_ref[st // 2, sub * ki + j], pb[j * TK:(j + 1) * TK])
            acc_s[st] = alpha * acc_s[st] + pv
            out.append((m_new, l_new))
        return tuple(out)

    buf_a = (s_a, mx_a)
    buf_b = (s_b, mx_b)

    def body(kk, carry):
        k0 = 2 * kk
        scores(k0 + 1, buf_b)
        carry = softmax_pv(k0, buf_a, carry, False)
        scores(k0 + 2, buf_a)
        return softmax_pv(k0 + 1, buf_b, carry, False)

    def last_even(carry):
        return softmax_pv(qi, buf_a, carry, True)

    def last_odd(carry):
        scores(qi, buf_b)
        return softmax_pv(qi, buf_b, softmax_pv(qi - 1, buf_a, carry, False), True)

    neg = jnp.full((1, TQ), NEG_INF, F32)
    zero_l = jnp.zeros((1, TQ), F32)
    scores(0, buf_a)
    carry = lax.fori_loop(0, qi // 2, body, ((neg, zero_l),) * n_streams)
    carry = lax.cond(qi % 2 == 0, last_even, last_odd, carry)

    for hl in range(HEADS_PER_STEP):
        ot = (acc_s[2 * hl] / carry[2 * hl][1]
              - lam * (acc_s[2 * hl + 1] / carry[2 * hl + 1][1]))
        o = ot.T
        o_ref[:, hl * V_HEAD_DIM:(hl + 1) * V_HEAD_DIM] = (
            _rms_rows(o, subg_ref[...]) * (1.0 - lam_init))


def _attn_core(page_table, q, kb, vt, qd, knd, vnd, cache_kt, cache_vf, layer,
               lq1, lk1, lq2, lk2, subg, *, batch, seq, lam_init):
    n, n_pages = page_table.shape
    nq = seq // TQ
    nk = seq // TK
    hps = HEADS_PER_STEP
    groups = N_HEADS // hps
    assert batch * groups * nq == n

    def lin(b, h, i):
        return (b * groups + h) * nq + i

    vec = lambda w: pl.BlockSpec((1, w), lambda b, h, i, pt: (0, 0))
    row = pl.BlockSpec((1, 1, D_MODEL), lambda b, h, i, pt: (lin(b, h, i), 0, 0))
    per_head = pl.BlockSpec((1, N_HEADS, V_HEAD_DIM), lambda b, h, i, pt: (lin(b, h, i), 0, 0))

    in_hbm = pl.BlockSpec(memory_space=pl.ANY)
    key = jnp.arange(PAGE_SIZE)
    expand = (key[:, None] == (jnp.arange(D_MODEL) // N_HEADS)[None, :]).astype(BF16)
    in_specs = ([vec(HEAD_DIM)] * 4 + [vec(V_HEAD_DIM)]
                + [pl.BlockSpec((TQ, hps * V_HEAD_DIM), lambda b, h, i, pt: (b * nq + i, h)),
                   pl.BlockSpec((seq, hps * V_HEAD_DIM), lambda b, h, i, pt: (b, h)),
                   pl.BlockSpec((hps, nk, V_HEAD_DIM, TK), lambda b, h, i, pt: (h, b, 0, 0)),
                   row, row, per_head,
                   pl.BlockSpec((PAGE_SIZE, D_MODEL), lambda b, h, i, pt: (0, 0)),
                   in_hbm, in_hbm])
    o, od = pl.pallas_call(
        functools.partial(_attn_core_kernel, layer=layer, n_pages=n_pages, lam_init=lam_init),
        out_shape=(jax.ShapeDtypeStruct((batch * seq, N_HEADS * V_HEAD_DIM), F32),
                   jax.ShapeDtypeStruct((n, N_HEADS, V_HEAD_DIM), F32)),
        grid_spec=pltpu.PrefetchScalarGridSpec(
            num_scalar_prefetch=1,
            grid=(batch, groups, nq),
            in_specs=in_specs,
            out_specs=(pl.BlockSpec((TQ, hps * V_HEAD_DIM), lambda b, h, i, pt: (b * nq + i, h)),
                       per_head),
            scratch_shapes=[pltpu.VMEM((2 * hps, TQ, V_HEAD_DIM), BF16),
                            pltpu.VMEM((2 * hps, TQ, TQ), F32),
                            pltpu.VMEM((2 * hps, TQ, TQ), F32),
                            pltpu.VMEM((2 * hps, 1, TQ), F32),
                            pltpu.VMEM((2 * hps, 1, TQ), F32),
                            pltpu.VMEM((2 * hps, V_HEAD_DIM, TQ), F32),
                            pltpu.VMEM((2, n_pages, D_MODEL, PAGE_SIZE), F32),
                            pltpu.SemaphoreType.DMA((2,))],
        ),
        compiler_params=_params("arbitrary", "arbitrary", "arbitrary"),
        name="attn_core",
    )(page_table, lq1, lk1, lq2, lk2, subg, q, kb, vt,
      qd.reshape(n, 1, D_MODEL), knd.reshape(n, 1, D_MODEL), vnd, expand, cache_kt, cache_vf)
    return o, od.reshape(n, D_MODEL)


def _decode_keys(lam, q_ref, kn_ref, k_refs):
    nseg = 2 * N_HEADS
    q = q_ref[0].astype(F32)
    seg = lax.broadcasted_iota(jnp.int32, (nseg, D_MODEL), 1) // HEAD_DIM
    rowi = lax.broadcasted_iota(jnp.int32, (nseg, D_MODEL), 0)
    row_seg = 2 * (rowi % N_HEADS) + rowi // N_HEADS
    qbd = jnp.where(seg == row_seg, jnp.broadcast_to(q, (nseg, D_MODEL)), 0.0)
    qbd16 = qbd.astype(BF16)
    s = jnp.concatenate([_dot(qbd16, k_ref[...].astype(BF16)) for k_ref in k_refs],
                        axis=1)
    kn = kn_ref[0].astype(F32)
    s_own = jnp.sum(qbd * kn, axis=-1, keepdims=True)
    m = jnp.maximum(jnp.max(s, axis=-1, keepdims=True), s_own)
    p = jnp.exp2(s - m)
    p_own = jnp.exp2(s_own - m)
    inv_l = 1.0 / (jnp.sum(p, axis=-1, keepdims=True) + p_own)
    p = p * inv_l
    p_own = p_own * inv_l
    return p[:N_HEADS] - lam * p[N_HEADS:], p_own[:N_HEADS] - lam * p_own[N_HEADS:]


def _decode_values(subg_ref, vn_ref, expand_ref, v_refs, a, a_own, o_ref, *, lam_init):
    n_pages = len(v_refs)
    a_rows = jnp.concatenate([a[:, i * PAGE_SIZE:(i + 1) * PAGE_SIZE] for i in range(n_pages)],
                             axis=0).astype(BF16)
    w = _dot(a_rows, expand_ref[...])
    same_head = (lax.broadcasted_iota(jnp.int32, w.shape, 0) % N_HEADS
                 == lax.broadcasted_iota(jnp.int32, w.shape, 1) % N_HEADS)
    w = jnp.where(same_head, w, 0.0).astype(BF16)
    vn = vn_ref[0].astype(BF16).astype(F32)
    o = a_own.astype(BF16).astype(F32) * vn
    for i, v_ref in enumerate(v_refs):
        o = o + _dot(w[i * N_HEADS:(i + 1) * N_HEADS], v_ref[...].astype(BF16))
    o_ref[0] = _rms_rows(o, subg_ref[...]) * (1.0 - lam_init)


def _attn_out_kernel(x_ref, o_ref, z_ref, w_ref, y_ref):
    gated = (o_ref[...] * _silu(z_ref[...])).astype(BF16)
    y_ref[...] = x_ref[...] + _dot(gated, w_ref[...])


def _attn_out(x2d, o, z, w_out, *, tm):
    r, d = x2d.shape
    row = pl.BlockSpec((tm, d), lambda i: (i, 0))
    return pl.pallas_call(
        _attn_out_kernel,
        out_shape=jax.ShapeDtypeStruct((r, d), F32),
        grid=(r // tm,),
        in_specs=[row, row, row, pl.BlockSpec((d, d), lambda i: (0, 0))],
        out_specs=row,
        compiler_params=_params("arbitrary"),
        name="attn_out",
    )(x2d, o, z, w_out)


def _rope_tables(pos):
    half = HEAD_DIM // 2
    inv = jnp.power(ROPE_THETA, -jnp.arange(0, HEAD_DIM, 2, dtype=F32) / HEAD_DIM)
    ang = pos.astype(F32)[:, None] * inv[None, :]
    cos = jnp.cos(ang)
    sin = jnp.sin(ang)
    cos_full = jnp.tile(cos, (1, V_HEAD_DIM // half))
    sin_full = jnp.tile(jnp.concatenate([-sin, sin], axis=-1), (1, V_HEAD_DIM // HEAD_DIM))
    return cos_full, sin_full


def kernel(x_prompt, x_sample, state_pool, cache_k, cache_v, page_table, norm_pool, w_in_pool, w_grp_pool, scale_pool, w_out_pool, norm_attn, w_in_attn, q_norm, k_norm, lambda_q1, lambda_k1, lambda_q2, lambda_k2, subln, w_out_attn):
    batch, seq, d = x_prompt.shape
    n_dec = x_sample.shape[0]
    n_attn = norm_attn.shape[0]
    depth = norm_pool.shape[0] + n_attn
    past_len = page_table.shape[1] * PAGE_SIZE
    n_phys = cache_k.shape[1]

    cos_p, sin_p = _rope_tables(jnp.arange(seq))
    cos_s, sin_s = _rope_tables(jnp.full((n_dec,), past_len, jnp.int32))
    seg = jnp.arange(256) // HEAD_DIM
    ones_bd = (seg[:, None] == seg[None, :]).astype(BF16)
    n_layers = cache_k.shape[0]
    cache_kt = jnp.transpose(cache_k, (0, 1, 3, 4, 5, 2)).reshape(n_layers, n_phys, d, PAGE_SIZE)
    cache_vf = cache_v.reshape(n_layers, n_phys, PAGE_SIZE * N_HEADS, V_HEAD_DIM)
    state_t = jnp.transpose(state_pool, (0, 2, 1, 3))

    xp = x_prompt
    xs = x_sample.reshape(n_dec, d)
    ps_prompt, ps_sample, kp, vp, ksm, vsm = [], [], [], [], [], []
    for i in range(depth):
        j = i // 2
        if i % 2 == 0:
            g = norm_pool[j].reshape(1, d)
            w_in = w_in_pool[j].astype(BF16)
            w_grp = w_grp_pool[j].astype(BF16)
            scale = scale_pool[j].reshape(1, POOL_WIDTH)
            w_out = w_out_pool[j].astype(BF16)
            xp, st_p = _pool_prompt(xp, g, w_in, w_grp, scale, w_out)
            xs, st_s = _pool_sample(xs, state_t, j, g, w_in, w_grp, scale, w_out)
            ps_prompt.append(st_p)
            ps_sample.append(st_s)
        else:
            lam_init = 0.8 - 0.6 * math.exp(-0.3 * i)
            g = norm_attn[j].reshape(1, d)
            w_in = w_in_attn[j].astype(BF16)
            w_out = w_out_attn[j].astype(BF16)
            qn = jnp.tile(q_norm[j], d // HEAD_DIM).reshape(1, d)
            kn = jnp.tile(k_norm[j], d // HEAD_DIM).reshape(1, d)
            lams = [a[j].reshape(1, HEAD_DIM) for a in (lambda_q1, lambda_k1, lambda_q2, lambda_k2)]
            subg = subln[j].reshape(1, V_HEAD_DIM)

            x2 = xp.reshape(batch * seq, d)
            last = j == n_attn - 1
            prev_p = (kp, vp) if last else ((), ())
            prev_s = (ksm, vsm) if last else ((), ())
            q, kb, kt, v, vt, z = _attn_proj(x2, g, w_in, ones_bd, qn, kn, cos_p, sin_p, *prev_p,
                                             tm=TM_PROJ, pos_tiles=seq // TM_PROJ)
            qs, kb_s, kt_s, v_s, _, z_s = _attn_proj(xs, g, w_in, ones_bd, qn, kn, cos_s, sin_s,
                                                     *prev_s, tm=n_dec, pos_tiles=1)
            o, o_s = _attn_core(page_table, q, kb, vt, qs, kb_s, v_s[-1], cache_kt, cache_vf, j,
                                *lams, subg, batch=batch, seq=seq, lam_init=lam_init)
            xp = _attn_out(x2, o, z, w_out, tm=512).reshape(batch, seq, d)
            xs = _attn_out(xs, o_s, z_s, w_out, tm=n_dec)
            kp.append(kt)
            vp.append(v)
            ksm.append(kt_s)
            vsm.append(v_s)
    ps_sample = jnp.transpose(jnp.concatenate(ps_sample, axis=0), (0, 2, 1, 3))
    k_prompt = jnp.transpose(kp[-1].reshape(n_attn, batch, N_HEADS, 2, HEAD_DIM, seq),
                             (0, 1, 5, 2, 3, 4))
    k_sample = jnp.transpose(ksm[-1].reshape(n_attn, 1, N_HEADS, 2, HEAD_DIM, n_dec),
                             (0, 5, 1, 2, 3, 4))
    v_prompt = vp[-1].reshape(n_attn, batch, seq, N_HEADS, V_HEAD_DIM)
    v_sample = vsm[-1].reshape(n_attn, n_dec, 1, N_HEADS, V_HEAD_DIM)
    return (xp, xs.reshape(n_dec, 1, d), jnp.stack(ps_prompt), ps_sample,
            k_prompt, v_prompt, k_sample, v_sample)
```

```python
import functools
import math

import jax
import jax.numpy as jnp
from jax import lax
from jax.experimental import pallas as pl
from jax.experimental.pallas import tpu as pltpu

D_MODEL = 1024
POOL_WIDTH = 2048
POOL_WINDOWS = (2, 4, 8, 16)
POOL_GROUP_WIDTH = 512
POOL_STATE_LEN = 15
N_HEADS = 8
HEAD_DIM = 64
V_HEAD_DIM = 128
PAGE_SIZE = 128
ROPE_THETA = 10000.0
RMS_EPS = 1e-6
NEG_INF = -1e30

VMEM_LIMIT_BYTES = 56 * 1024 * 1024
HALO = 16
TM_PROJ = 256
TM_POOL = 512
TQ = 512
TK = TM_PROJ
HEADS_PER_STEP = 2

F32 = jnp.float32
BF16 = jnp.bfloat16


def _params(*sem):
    return pltpu.CompilerParams(dimension_semantics=sem, vmem_limit_bytes=VMEM_LIMIT_BYTES)


def _rms_rows(x, g):
    ms = jnp.mean(x * x, axis=-1, keepdims=True)
    return x * lax.rsqrt(ms + RMS_EPS) * g


def _silu(z):
    return z * jax.nn.sigmoid(z)


def _dot(a, b):
    return jnp.dot(a, b, preferred_element_type=F32)


def _pool_prompt_kernel(x_ref, g_ref, win_ref, wgrp_ref, scale_ref, wout_ref,
                        y_ref, st_ref, ubuf, *, tm):
    t = pl.program_id(1)

    @pl.when(t == 0)
    def _():
        ubuf[0:HALO, :] = jnp.zeros((HALO, POOL_WIDTH), F32)

    x = x_ref[0]
    h = _rms_rows(x, g_ref[...]).astype(BF16)
    uz = _dot(h, win_ref[...])
    u = uz[:, :POOL_WIDTH]
    z = uz[:, POOL_WIDTH:]
    ubuf[HALO:HALO + tm, :] = u

    pos = t * tm + lax.broadcasted_iota(jnp.int32, (tm, 1), 0)
    ms = []
    for gi, w in enumerate(POOL_WINDOWS):
        cols = slice(gi * POOL_GROUP_WIDTH, (gi + 1) * POOL_GROUP_WIDTH)
        acc = ubuf[HALO:HALO + tm, cols]
        for i in range(1, w):
            acc = acc + ubuf[HALO - i:HALO - i + tm, cols]
        inv_cnt = 1.0 / jnp.minimum(pos + 1, w).astype(F32)
        r = acc * inv_cnt - u[:, cols]
        ms.append(_dot(r.astype(BF16), wgrp_ref[gi]))
    m = jnp.concatenate(ms, axis=-1) * scale_ref[...]
    gated = (m * _silu(z)).astype(BF16)
    y_ref[0] = x + _dot(gated, wout_ref[...])

    @pl.when(t == pl.num_programs(1) - 1)
    def _():
        st_ref[0] = ubuf[HALO + tm - POOL_STATE_LEN:HALO + tm, :]

    ubuf[0:HALO, :] = ubuf[tm:tm + HALO, :]


def _pool_prompt(x, g, w_in, w_grp, scale, w_out, tm=TM_POOL):
    b, s, d = x.shape
    const = lambda *shape: pl.BlockSpec(shape, lambda i, j: (0,) * len(shape),
                                        pipeline_mode=pl.Buffered(1))
    return pl.pallas_call(
        functools.partial(_pool_prompt_kernel, tm=tm),
        out_shape=(jax.ShapeDtypeStruct((b, s, d), F32),
                   jax.ShapeDtypeStruct((b, POOL_STATE_LEN, POOL_WIDTH), F32)),
        grid=(b, s // tm),
        in_specs=[
            pl.BlockSpec((1, tm, d), lambda i, j: (i, j, 0)),
            const(1, d),
            const(d, 2 * POOL_WIDTH),
            const(len(POOL_WINDOWS), POOL_GROUP_WIDTH, POOL_GROUP_WIDTH),
            const(1, POOL_WIDTH),
            const(POOL_WIDTH, d),
        ],
        out_specs=(pl.BlockSpec((1, tm, d), lambda i, j: (i, j, 0)),
                   pl.BlockSpec((1, POOL_STATE_LEN, POOL_WIDTH), lambda i, j: (i, 0, 0))),
        scratch_shapes=[pltpu.VMEM((tm + HALO, POOL_WIDTH), F32)],
        compiler_params=_params("arbitrary", "arbitrary"),
        name="pool_prompt",
    )(x, g, w_in, w_grp, scale, w_out)


def _pool_sample_kernel(x_ref, st_ref, g_ref, win_ref, wgrp_ref, scale_ref, wout_ref, *rest,
                        n_prev):
    prev = rest[:n_prev]
    y_ref, nst_ref = rest[n_prev:]
    for j in range(n_prev):
        nst_ref[j] = prev[j][0]
    x = x_ref[...]
    h = _rms_rows(x, g_ref[...]).astype(BF16)
    uz = _dot(h, win_ref[...])
    u = uz[:, :POOL_WIDTH]
    z = uz[:, POOL_WIDTH:]
    nst_ref[n_prev, 0:POOL_STATE_LEN - 1] = st_ref[0, 1:POOL_STATE_LEN]
    nst_ref[n_prev, POOL_STATE_LEN - 1] = u
    ms = []
    for gi, w in enumerate(POOL_WINDOWS):
        cols = slice(gi * POOL_GROUP_WIDTH, (gi + 1) * POOL_GROUP_WIDTH)
        acc = u[:, cols]
        for i in range(1, w):
            acc = acc + st_ref[0, POOL_STATE_LEN - i, :, cols]
        r = acc * (1.0 / w) - u[:, cols]
        ms.append(_dot(r.astype(BF16), wgrp_ref[gi]))
    m = jnp.concatenate(ms, axis=-1) * scale_ref[...]
    gated = (m * _silu(z)).astype(BF16)
    y_ref[...] = x + _dot(gated, wout_ref[...])


def _pool_sample(x, state_t, layer, g, w_in, w_grp, scale, w_out, prev=(), tb=32):
    n, d = x.shape
    n_prev = len(prev)
    const = lambda *shape: pl.BlockSpec(shape, lambda i: (0,) * len(shape),
                                        pipeline_mode=pl.Buffered(1))
    state_spec = lambda layers, at: pl.BlockSpec((layers, POOL_STATE_LEN, tb, POOL_WIDTH),
                                                 lambda i: (at, 0, i, 0))
    return pl.pallas_call(
        functools.partial(_pool_sample_kernel, n_prev=n_prev),
        out_shape=(jax.ShapeDtypeStruct((n, d), F32),
                   jax.ShapeDtypeStruct((n_prev + 1, POOL_STATE_LEN, n, POOL_WIDTH), F32)),
        grid=(n // tb,),
        in_specs=[
            pl.BlockSpec((tb, d), lambda i: (i, 0)),
            state_spec(1, layer),
            const(1, d),
            const(d, 2 * POOL_WIDTH),
            const(len(POOL_WINDOWS), POOL_GROUP_WIDTH, POOL_GROUP_WIDTH),
            const(1, POOL_WIDTH),
            const(POOL_WIDTH, d),
        ] + [state_spec(1, 0)] * n_prev,
        out_specs=(pl.BlockSpec((tb, d), lambda i: (i, 0)), state_spec(n_prev + 1, 0)),
        compiler_params=_params("arbitrary"),
        name="pool_sample",
    )(x, state_t, g, w_in, w_grp, scale, w_out, *prev)


def _seg_norm_rope(a, ones_ref, gain, cos, sin):
    sq = a * a
    hi = sq.astype(BF16)
    lo = (sq - hi.astype(F32)).astype(BF16)
    parts = []
    for c in range(a.shape[1] // 256):
        cs = slice(c * 256, (c + 1) * 256)
        parts.append(_dot(hi[:, cs], ones_ref[...]) + _dot(lo[:, cs], ones_ref[...]))
    ss = jnp.concatenate(parts, axis=-1)
    an = a * lax.rsqrt(ss * (1.0 / HEAD_DIM) + RMS_EPS) * gain
    lane = lax.broadcasted_iota(jnp.int32, (a.shape[0], V_HEAD_DIM), 1)
    first_half = (lane % HEAD_DIM) < (HEAD_DIM // 2)
    outs = []
    for hh in range(N_HEADS):
        xh = an[:, hh * V_HEAD_DIM:(hh + 1) * V_HEAD_DIM]
        swapped = jnp.where(first_half,
                            pltpu.roll(xh, V_HEAD_DIM - HEAD_DIM // 2, 1),
                            pltpu.roll(xh, HEAD_DIM // 2, 1))
        outs.append(xh * cos + swapped * sin)
    return jnp.concatenate(outs, axis=-1)


def _attn_proj_kernel(x_ref, g_ref, win_ref, ones_ref, qn_ref, kn_ref, cos_ref, sin_ref, *rest,
                      n_prev):
    prev_kt = rest[:n_prev]
    prev_v = rest[n_prev:2 * n_prev]
    q_ref, kb_ref, kt_ref, v_ref, vt_ref, z_ref = rest[2 * n_prev:]
    for j in range(n_prev):
        kt_ref[j] = prev_kt[j][0]
        v_ref[j] = prev_v[j][0]
    x = x_ref[...]
    h = _rms_rows(x, g_ref[...]).astype(BF16)
    cos = cos_ref[...]
    sin = sin_ref[...]
    v = _dot(h, win_ref[:, 2 * D_MODEL:3 * D_MODEL])
    v_ref[n_prev] = v.reshape(v.shape[0], N_HEADS, V_HEAD_DIM)
    vt_ref[:, 0] = v.T.astype(BF16).reshape(N_HEADS, V_HEAD_DIM, v.shape[0])
    k = _seg_norm_rope(_dot(h, win_ref[:, D_MODEL:2 * D_MODEL]), ones_ref, kn_ref[...], cos, sin)
    kb_ref[...] = k.astype(BF16)
    kt_ref[n_prev, 0] = k.T
    q = _seg_norm_rope(_dot(h, win_ref[:, 0:D_MODEL]), ones_ref, qn_ref[...], cos, sin)
    q_ref[...] = (q * (HEAD_DIM ** -0.5 * math.log2(math.e))).astype(BF16)
    z_ref[...] = _dot(h, win_ref[:, 3 * D_MODEL:])


def _attn_proj(x2d, g, w_in, ones, qn_full, kn_full, cos, sin, prev_kt=(), prev_v=(), *,
               tm, pos_tiles):
    r, d = x2d.shape
    nt = r // tm
    n_prev = len(prev_kt)
    const = lambda *shape: pl.BlockSpec(shape, lambda i: (0,) * len(shape))
    row = lambda width: pl.BlockSpec((tm, width), lambda i: (i, 0))
    tab = pl.BlockSpec((tm, V_HEAD_DIM), lambda i: (i % pos_tiles, 0))
    kt_spec = lambda layers: pl.BlockSpec((layers, 1, d, tm),
                                          lambda i: (0, i // pos_tiles, 0, i % pos_tiles))
    v_spec = lambda layers: pl.BlockSpec((layers, tm, N_HEADS, V_HEAD_DIM), lambda i: (0, i, 0, 0))
    return pl.pallas_call(
        functools.partial(_attn_proj_kernel, n_prev=n_prev),
        out_shape=(jax.ShapeDtypeStruct((r, d), BF16),
                   jax.ShapeDtypeStruct((r, d), BF16),
                   jax.ShapeDtypeStruct((n_prev + 1, nt // pos_tiles, d, pos_tiles * tm), F32),
                   jax.ShapeDtypeStruct((n_prev + 1, r, N_HEADS, V_HEAD_DIM), F32),
                   jax.ShapeDtypeStruct((N_HEADS, nt, V_HEAD_DIM, tm), BF16),
                   jax.ShapeDtypeStruct((r, d), F32)),
        grid=(nt,),
        in_specs=([row(d), const(1, d), const(d, 4 * d), const(256, 256),
                   const(1, d), const(1, d), tab, tab]
                  + [kt_spec(1)] * n_prev + [v_spec(1)] * n_prev),
        out_specs=(row(d), row(d), kt_spec(n_prev + 1), v_spec(n_prev + 1),
                   pl.BlockSpec((N_HEADS, 1, V_HEAD_DIM, tm), lambda i: (0, i, 0, 0)),
                   row(d)),
        compiler_params=_params("arbitrary"),
        name="attn_proj",
    )(x2d, g, w_in, ones, qn_full, kn_full, cos, sin, *prev_kt, *prev_v)


def _lambda(lq1, lk1, lq2, lk2, lam_init):
    a = jnp.sum(lq1 * lk1, axis=-1, keepdims=True)
    b = jnp.sum(lq2 * lk2, axis=-1, keepdims=True)
    return jnp.exp(a) - jnp.exp(b) + lam_init


def _attn_core_kernel(pt_ref, lq1_ref, lk1_ref, lq2_ref, lk2_ref, subg_ref, q_ref, k_ref, vt_ref,
                      qd_ref, knd_ref, vnd_ref, expand_ref, kcache_ref, vcache_ref,
                      o_ref, od_ref, qc_s, s_a, s_b, mx_a, mx_b, acc_s, pages, sem,
                      *, layer, n_pages, lam_init):
    lam = _lambda(lq1_ref[...], lk1_ref[...], lq2_ref[...], lk2_ref[...], lam_init)
    n_steps = pl.num_programs(0) * pl.num_programs(1) * pl.num_programs(2)
    step = (pl.program_id(0) * pl.num_programs(1) + pl.program_id(1)) * pl.num_programs(2) \
        + pl.program_id(2)
    key_pages = [pages.at[0, i] for i in range(n_pages)]
    value_pages = [pages.at[1, i] for i in range(n_pages)]

    def key_copy(r, i):
        return pltpu.make_async_copy(kcache_ref.at[layer, pt_ref[r, i]], key_pages[i], sem.at[0])

    def value_copy(r, i):
        return pltpu.make_async_copy(vcache_ref.at[layer, pt_ref[r, i]], value_pages[i], sem.at[1])

    @pl.when(step == 0)
    def _():
        for i in range(n_pages):
            key_copy(0, i).start()
        for i in range(n_pages):
            value_copy(0, i).start()

    for i in range(n_pages):
        key_copy(step, i).wait()
    a, a_own = _decode_keys(lam, qd_ref, knd_ref, key_pages)

    @pl.when(step + 1 < n_steps)
    def _():
        for i in range(n_pages):
            key_copy(step + 1, i).start()

    for i in range(n_pages):
        value_copy(step, i).wait()
    _decode_values(subg_ref, vnd_ref, expand_ref, value_pages, a, a_own, od_ref, lam_init=lam_init)

    @pl.when(step + 1 < n_steps)
    def _():
        for i in range(n_pages):
            value_copy(step + 1, i).start()

    _flash_tile(lam, subg_ref, q_ref, k_ref, vt_ref, o_ref, qc_s, s_a, s_b, mx_a, mx_b, acc_s,
                lam_init=lam_init)


def _flash_tile(lam, subg_ref, q_ref, k_ref, vt_ref, o_ref, qc_s, s_a, s_b, mx_a, mx_b, acc_s,
                *, lam_init):
    qi = pl.program_id(2)
    n_streams = 2 * HEADS_PER_STEP
    lane = lax.broadcasted_iota(jnp.int32, (TQ, V_HEAD_DIM), 1)
    for hl in range(HEADS_PER_STEP):
        q = q_ref[:, hl * V_HEAD_DIM:(hl + 1) * V_HEAD_DIM]
        zero = jnp.zeros_like(q)
        qc_s[2 * hl] = jnp.where(lane < HEAD_DIM, q, zero)
        qc_s[2 * hl + 1] = jnp.where(lane >= HEAD_DIM, q, zero)
    acc_s[...] = jnp.zeros(acc_s.shape, F32)

    def scores(ki, sm_ref):
        s_ref, mx_ref = sm_ref
        for hl in range(HEADS_PER_STEP):
            k = k_ref[pl.ds(pl.multiple_of(ki * TQ, TQ), TQ), hl * V_HEAD_DIM:(hl + 1) * V_HEAD_DIM]
            for st in (2 * hl, 2 * hl + 1):
                s = lax.dot_general(k, qc_s[st], (((1,), (1,)), ((), ())),
                                    preferred_element_type=F32)
                s_ref[st] = s
                mx_ref[st] = jnp.max(s, axis=0, keepdims=True)

    def softmax_pv(ki, sm_ref, carry, masked):
        s_ref, mx_ref = sm_ref
        if masked:
            keep = (lax.broadcasted_iota(jnp.int32, (TQ, TQ), 0)
                    <= lax.broadcasted_iota(jnp.int32, (TQ, TQ), 1))
        out = []
        for st in range(n_streams):
            m_old, l_old = carry[st]
            s = s_ref[st]
            if masked:
                s = jnp.where(keep, s, NEG_INF)
                m_tile = jnp.max(s, axis=0, keepdims=True)
            else:
                m_tile = mx_ref[st]
            m_new = jnp.maximum(m_old, m_tile)
            alpha = jnp.exp2(m_old - m_new)
            p = jnp.exp2(s - m_new)
            l_new = alpha * l_old + jnp.sum(p, axis=0, keepdims=True)
            pb = p.astype(BF16)
            sub = TQ // TK
            pv = _dot(vt_ref[st // 2, sub * ki], pb[0:TK])
            for j in range(1, sub):
                pv = pv + _dot(vt_ref[st // 2, sub * ki + j], pb[j * TK:(j + 1) * TK])
            acc_s[st] = alpha * acc_s[st] + pv
            out.append((m_new, l_new))
        return tuple(out)

    buf_a = (s_a, mx_a)
    buf_b = (s_b, mx_b)

    def body(kk, carry):
        k0 = 2 * kk
        scores(k0 + 1, buf_b)
        carry = softmax_pv(k0, buf_a, carry, False)
        scores(k0 + 2, buf_a)
        return softmax_pv(k0 + 1, buf_b, carry, False)

    def last_even(carry):
        return softmax_pv(qi, buf_a, carry, True)

    def last_odd(carry):
        scores(qi, buf_b)
        return softmax_pv(qi, buf_b, softmax_pv(qi - 1, buf_a, carry, False), True)

    neg = jnp.full((1, TQ), NEG_INF, F32)
    zero_l = jnp.zeros((1, TQ), F32)
    scores(0, buf_a)
    carry = lax.fori_loop(0, qi // 2, body, ((neg, zero_l),) * n_streams)
    carry = lax.cond(qi % 2 == 0, last_even, last_odd, carry)

    for hl in range(HEADS_PER_STEP):
        ot = (acc_s[2 * hl] / carry[2 * hl][1]
              - lam * (acc_s[2 * hl + 1] / carry[2 * hl + 1][1]))
        o = ot.T
        o_ref[:, hl * V_HEAD_DIM:(hl + 1) * V_HEAD_DIM] = (
            _rms_rows(o, subg_ref[...]) * (1.0 - lam_init))


def _attn_core(page_table, q, kb, vt, qd, knd, vnd, cache_kt, cache_vf, layer,
               lq1, lk1, lq2, lk2, subg, *, batch, seq, lam_init):
    n, n_pages = page_table.shape
    nq = seq // TQ
    nk = seq // TK
    hps = HEADS_PER_STEP
    groups = N_HEADS // hps
    assert batch * groups * nq == n

    def lin(b, h, i):
        return (b * groups + h) * nq + i

    vec = lambda w: pl.BlockSpec((1, w), lambda b, h, i, pt: (0, 0))
    row = pl.BlockSpec((1, 1, D_MODEL), lambda b, h, i, pt: (lin(b, h, i), 0, 0))
    per_head = pl.BlockSpec((1, N_HEADS, V_HEAD_DIM), lambda b, h, i, pt: (lin(b, h, i), 0, 0))

    in_hbm = pl.BlockSpec(memory_space=pl.ANY)
    key = jnp.arange(PAGE_SIZE)
    expand = (key[:, None] == (jnp.arange(D_MODEL) // N_HEADS)[None, :]).astype(BF16)
    in_specs = ([vec(HEAD_DIM)] * 4 + [vec(V_HEAD_DIM)]
                + [pl.BlockSpec((TQ, hps * V_HEAD_DIM), lambda b, h, i, pt: (b * nq + i, h)),
                   pl.BlockSpec((seq, hps * V_HEAD_DIM), lambda b, h, i, pt: (b, h)),
                   pl.BlockSpec((hps, nk, V_HEAD_DIM, TK), lambda b, h, i, pt: (h, b, 0, 0)),
                   row, row, per_head,
                   pl.BlockSpec((PAGE_SIZE, D_MODEL), lambda b, h, i, pt: (0, 0)),
                   in_hbm, in_hbm])
    o, od = pl.pallas_call(
        functools.partial(_attn_core_kernel, layer=layer, n_pages=n_pages, lam_init=lam_init),
        out_shape=(jax.ShapeDtypeStruct((batch * seq, N_HEADS * V_HEAD_DIM), F32),
                   jax.ShapeDtypeStruct((n, N_HEADS, V_HEAD_DIM), F32)),
        grid_spec=pltpu.PrefetchScalarGridSpec(
            num_scalar_prefetch=1,
            grid=(batch, groups, nq),
            in_specs=in_specs,
            out_specs=(pl.BlockSpec((TQ, hps * V_HEAD_DIM), lambda b, h, i, pt: (b * nq + i, h)),
                       per_head),
            scratch_shapes=[pltpu.VMEM((2 * hps, TQ, V_HEAD_DIM), BF16),
                            pltpu.VMEM((2 * hps, TQ, TQ), F32),
                            pltpu.VMEM((2 * hps, TQ, TQ), F32),
                            pltpu.VMEM((2 * hps, 1, TQ), F32),
                            pltpu.VMEM((2 * hps, 1, TQ), F32),
                            pltpu.VMEM((2 * hps, V_HEAD_DIM, TQ), F32),
                            pltpu.VMEM((2, n_pages, D_MODEL, PAGE_SIZE), F32),
                            pltpu.SemaphoreType.DMA((2,))],
        ),
        compiler_params=_params("arbitrary", "arbitrary", "arbitrary"),
        name="attn_core",
    )(page_table, lq1, lk1, lq2, lk2, subg, q, kb, vt,
      qd.reshape(n, 1, D_MODEL), knd.reshape(n, 1, D_MODEL), vnd, expand, cache_kt, cache_vf)
    return o, od.reshape(n, D_MODEL)


def _decode_keys(lam, q_ref, kn_ref, k_refs):
    nseg = 2 * N_HEADS
    q = q_ref[0].astype(F32)
    seg = lax.broadcasted_iota(jnp.int32, (nseg, D_MODEL), 1) // HEAD_DIM
    rowi = lax.broadcasted_iota(jnp.int32, (nseg, D_MODEL), 0)
    row_seg = 2 * (rowi % N_HEADS) + rowi // N_HEADS
    qbd = jnp.where(seg == row_seg, jnp.broadcast_to(q, (nseg, D_MODEL)), 0.0)
    qbd16 = qbd.astype(BF16)
    s = jnp.concatenate([_dot(qbd16, k_ref[...].astype(BF16)) for k_ref in k_refs],
                        axis=1)
    kn = kn_ref[0].astype(F32)
    s_own = jnp.sum(qbd * kn, axis=-1, keepdims=True)
    m = jnp.maximum(jnp.max(s, axis=-1, keepdims=True), s_own)
    p = jnp.exp2(s - m)
    p_own = jnp.exp2(s_own - m)
    inv_l = 1.0 / (jnp.sum(p, axis=-1, keepdims=True) + p_own)
    p = p * inv_l
    p_own = p_own * inv_l
    return p[:N_HEADS] - lam * p[N_HEADS:], p_own[:N_HEADS] - lam * p_own[N_HEADS:]


def _decode_values(subg_ref, vn_ref, expand_ref, v_refs, a, a_own, o_ref, *, lam_init):
    n_pages = len(v_refs)
    a_rows = jnp.concatenate([a[:, i * PAGE_SIZE:(i + 1) * PAGE_SIZE] for i in range(n_pages)],
                             axis=0).astype(BF16)
    w = _dot(a_rows, expand_ref[...])
    same_head = (lax.broadcasted_iota(jnp.int32, w.shape, 0) % N_HEADS
                 == lax.broadcasted_iota(jnp.int32, w.shape, 1) % N_HEADS)
    w = jnp.where(same_head, w, 0.0).astype(BF16)
    vn = vn_ref[0].astype(BF16).astype(F32)
    o = a_own.astype(BF16).astype(F32) * vn
    for i, v_ref in enumerate(v_refs):
        o = o + _dot(w[i * N_HEADS:(i + 1) * N_HEADS], v_ref[...].astype(BF16))
    o_ref[0] = _rms_rows(o, subg_ref[...]) * (1.0 - lam_init)


def _attn_out_kernel(x_ref, o_ref, z_ref, w_ref, y_ref):
    gated = (o_ref[...] * _silu(z_ref[...])).astype(BF16)
    y_ref[...] = x_ref[...] + _dot(gated, w_ref[...])


def _attn_out(x2d, o, z, w_out, *, tm):
    r, d = x2d.shape
    row = pl.BlockSpec((tm, d), lambda i: (i, 0))
    return pl.pallas_call(
        _attn_out_kernel,
        out_shape=jax.ShapeDtypeStruct((r, d), F32),
        grid=(r // tm,),
        in_specs=[row, row, row, pl.BlockSpec((d, d), lambda i: (0, 0))],
        out_specs=row,
        compiler_params=_params("arbitrary"),
        name="attn_out",
    )(x2d, o, z, w_out)


def _rope_tables(pos):
    half = HEAD_DIM // 2
    inv = jnp.power(ROPE_THETA, -jnp.arange(0, HEAD_DIM, 2, dtype=F32) / HEAD_DIM)
    ang = pos.astype(F32)[:, None] * inv[None, :]
    cos = jnp.cos(ang)
    sin = jnp.sin(ang)
    cos_full = jnp.tile(cos, (1, V_HEAD_DIM // half))
    sin_full = jnp.tile(jnp.concatenate([-sin, sin], axis=-1), (1, V_HEAD_DIM // HEAD_DIM))
    return cos_full, sin_full


def kernel(x_prompt, x_sample, state_pool, cache_k, cache_v, page_table, norm_pool, w_in_pool, w_grp_pool, scale_pool, w_out_pool, norm_attn, w_in_attn, q_norm, k_norm, lambda_q1, lambda_k1, lambda_q2, lambda_k2, subln, w_out_attn):
    batch, seq, d = x_prompt.shape
    n_dec = x_sample.shape[0]
    n_attn = norm_attn.shape[0]
    depth = norm_pool.shape[0] + n_attn
    past_len = page_table.shape[1] * PAGE_SIZE
    n_phys = cache_k.shape[1]

    cos_p, sin_p = _rope_tables(jnp.arange(seq))
    cos_s, sin_s = _rope_tables(jnp.full((n_dec,), past_len, jnp.int32))
    seg = jnp.arange(256) // HEAD_DIM
    ones_bd = (seg[:, None] == seg[None, :]).astype(BF16)
    n_layers = cache_k.shape[0]
    cache_kt = jnp.transpose(cache_k, (0, 1, 3, 4, 5, 2)).reshape(n_layers, n_phys, d, PAGE_SIZE)
    cache_vf = cache_v.reshape(n_layers, n_phys, PAGE_SIZE * N_HEADS, V_HEAD_DIM)
    state_t = jnp.transpose(state_pool, (0, 2, 1, 3))

    xp = x_prompt
    xs = x_sample.reshape(n_dec, d)
    ps_prompt, ps_sample, kp, vp, ksm, vsm = [], [], [], [], [], []
    for i in range(depth):
        j = i // 2
        if i % 2 == 0:
            g = norm_pool[j].reshape(1, d)
            w_in = w_in_pool[j].astype(BF16)
            w_grp = w_grp_pool[j].astype(BF16)
            scale = scale_pool[j].reshape(1, POOL_WIDTH)
            w_out = w_out_pool[j].astype(BF16)
            xp, st_p = _pool_prompt(xp, g, w_in, w_grp, scale, w_out)
            prev_states = ps_sample if j == norm_pool.shape[0] - 1 else ()
            xs, st_s = _pool_sample(xs, state_t, j, g, w_in, w_grp, scale, w_out, prev_states)
            ps_prompt.append(st_p)
            ps_sample.append(st_s)
        else:
            lam_init = 0.8 - 0.6 * math.exp(-0.3 * i)
            g = norm_attn[j].reshape(1, d)
            w_in = w_in_attn[j].astype(BF16)
            w_out = w_out_attn[j].astype(BF16)
            qn = jnp.tile(q_norm[j], d // HEAD_DIM).reshape(1, d)
            kn = jnp.tile(k_norm[j], d // HEAD_DIM).reshape(1, d)
            lams = [a[j].reshape(1, HEAD_DIM) for a in (lambda_q1, lambda_k1, lambda_q2, lambda_k2)]
            subg = subln[j].reshape(1, V_HEAD_DIM)

            x2 = xp.reshape(batch * seq, d)
            last = j == n_attn - 1
            prev_p = (kp, vp) if last else ((), ())
            prev_s = (ksm, vsm) if last else ((), ())
            q, kb, kt, v, vt, z = _attn_proj(x2, g, w_in, ones_bd, qn, kn, cos_p, sin_p, *prev_p,
                                             tm=TM_PROJ, pos_tiles=seq // TM_PROJ)
            qs, kb_s, kt_s, v_s, _, z_s = _attn_proj(xs, g, w_in, ones_bd, qn, kn, cos_s, sin_s,
                                                     *prev_s, tm=n_dec, pos_tiles=1)
            o, o_s = _attn_core(page_table, q, kb, vt, qs, kb_s, v_s[-1], cache_kt, cache_vf, j,
                                *lams, subg, batch=batch, seq=seq, lam_init=lam_init)
            xp = _attn_out(x2, o, z, w_out, tm=512).reshape(batch, seq, d)
            xs = _attn_out(xs, o_s, z_s, w_out, tm=n_dec)
            kp.append(kt)
            vp.append(v)
            ksm.append(kt_s)
            vsm.append(v_s)
    ps_sample = jnp.transpose(ps_sample[-1], (0, 2, 1, 3))
    k_prompt = jnp.transpose(kp[-1].reshape(n_attn, batch, N_HEADS, 2, HEAD_DIM, seq),
                             (0, 1, 5, 2, 3, 4))
    k_sample = jnp.transpose(ksm[-1].reshape(n_attn, 1, N_HEADS, 2, HEAD_DIM, n_dec),
                             (0, 5, 1, 2, 3, 4))
    v_prompt = vp[-1].reshape(n_attn, batch, seq, N_HEADS, V_HEAD_DIM)
    v_sample = vsm[-1].reshape(n_attn, n_dec, 1, N_HEADS, V_HEAD_DIM)
    return (xp, xs.reshape(n_dec, 1, d), jnp.stack(ps_prompt), ps_sample,
            k_prompt, v_prompt, k_sample, v_sample)
```
